```python
import jax, jax.numpy as jnp
from jax import lax
import numpy as np

D_MODEL = 2048
BATCH = 16
SEQ = 256
DEPTH = 4
DEC_BATCH = 8
DEC_SEQ = 2048
PAST_LEN = 512

GRID_W = 64
CHUNK = 128
D_FF = 5632
N_MOD = 9
EPS = 1e-6
ROPE_THETA = 10000.0

CM_GROUPS = 8
CM_GROUP_CH = 128
CM_WIDTH = CM_GROUPS * CM_GROUP_CH
RET_HEADS = 4
RET_DK = 128
RET_DV = 256
RET_QK_WIDTH = RET_HEADS * RET_DK
RET_V_WIDTH = RET_HEADS * RET_DV
RET_DECAY_EXP0 = 5
MLA_HEADS = 8
MLA_NOPE = 128
MLA_ROPE = 64
MLA_V = 128
MLA_Q_RANK = 512
MLA_KV_RANK = 512
MLA_QK_DIM = MLA_NOPE + MLA_ROPE
MLA_WIDTH = MLA_HEADS * MLA_V

N_BRANCHES = 3
BRANCH_WIDTH = 1024
IN_SIZES = (CM_WIDTH, CM_WIDTH, RET_QK_WIDTH, RET_QK_WIDTH, RET_V_WIDTH, RET_V_WIDTH,
            MLA_Q_RANK, MLA_KV_RANK, MLA_ROPE, N_BRANCHES * D_MODEL)
IN_WIDTH = sum(IN_SIZES)

kernel_name = 'hybrid_diffusion_prefix_trunk_step'


def rms_norm(x, w=None):
    xf = x.astype(jnp.float32)
    y = xf * lax.rsqrt(jnp.mean(xf * xf, axis=-1, keepdims=True) + EPS)
    if w is not None:
        y = y * w.astype(jnp.float32)
    return y.astype(x.dtype)


def modulate(x, shift, scale):
    return x * (1 + scale) + shift


def swiglu(x, w_gate_up, w_down):
    gate, up = jnp.split(x @ w_gate_up, 2, axis=-1)
    return (jax.nn.silu(gate) * up) @ w_down


def axial_rope_tables(n_tokens):
    rows = n_tokens // GRID_W
    t = jnp.arange(rows * GRID_W)
    row = (t // GRID_W).astype(jnp.float32)
    col = (t % GRID_W).astype(jnp.float32)
    n_freq = MLA_ROPE // 4
    freqs = jnp.power(ROPE_THETA, -jnp.arange(n_freq, dtype=jnp.float32) / n_freq)
    ang = jnp.concatenate([row[:, None] * freqs, col[:, None] * freqs], axis=-1)
    return jnp.cos(ang), jnp.sin(ang)


def rope_tail(x, cos, sin):
    x_pass, x_rot = jnp.split(x, [MLA_NOPE], axis=-1)
    x1, x2 = jnp.split(x_rot.astype(jnp.float32), 2, axis=-1)
    rot = jnp.concatenate([x1 * cos - x2 * sin, x2 * cos + x1 * sin], axis=-1)
    return jnp.concatenate([x_pass, rot.astype(x.dtype)], axis=-1)


def block_attention(q, k, v):
    b, lq, h, dq = q.shape
    nblk = lq // CHUNK
    qb = q.reshape(b, nblk, CHUNK, h, dq).transpose(1, 0, 2, 3, 4)
    scale = dq ** -0.5

    def one_block(qi):
        s = jnp.einsum('bqhd,bkhd->bhqk', qi, k).astype(jnp.float32) * scale
        p = jax.nn.softmax(s, axis=-1)
        return jnp.einsum('bhqk,bkhd->bqhd', p.astype(v.dtype), v)

    out = lax.map(one_block, qb)
    return out.transpose(1, 0, 2, 3, 4).reshape(b, lq, h, v.shape[-1])


def retention_scan(q, k, v, log_gamma, s0):
    b, l, h, dk = q.shape
    dv = v.shape[-1]
    n = l // CHUNK

    def to_chunks(x):
        return x.astype(jnp.float32).reshape(b, n, CHUNK, h, x.shape[-1]).transpose(1, 0, 3, 2, 4)

    qc, kc, vc = to_chunks(q), to_chunks(k), to_chunks(v)
    pos = jnp.arange(CHUNK, dtype=jnp.float32)
    lg = log_gamma[:, None]
    rel = pos[:, None] - pos[None, :]
    intra_decay = jnp.where(rel >= 0, jnp.exp(lg[:, :, None] * jnp.maximum(rel, 0.0)), 0.0)
    q_decay = jnp.exp(lg * (pos + 1.0))[:, :, None]
    k_decay = jnp.exp(lg * (CHUNK - 1.0 - pos))[:, :, None]
    chunk_decay = jnp.exp(log_gamma * CHUNK)[:, None, None]

    def step(state, chunk):
        qi, ki, vi = chunk
        scores = jnp.einsum('bhik,bhjk->bhij', qi, ki) * intra_decay
        out = (jnp.einsum('bhij,bhjv->bhiv', scores, vi)
               + jnp.einsum('bhik,bhkv->bhiv', qi * q_decay, state))
        state = chunk_decay * state + jnp.einsum('bhjk,bhjv->bhkv', ki * k_decay, vi)
        return state, out

    s_final, out = lax.scan(step, s0.astype(jnp.float32), (qc, kc, vc))
    return out.transpose(1, 0, 3, 2, 4).reshape(b, l, h, dv), s_final


def retention_branch(r_q, r_k, r_v, r_g, decay_logit, norm_w, s0):
    b, l, _ = r_q.shape
    q = r_q.reshape(b, l, RET_HEADS, RET_DK)
    k = r_k.reshape(b, l, RET_HEADS, RET_DK) * (RET_DK ** -0.5)
    v = r_v.reshape(b, l, RET_HEADS, RET_DV)
    log_gamma = jax.nn.log_sigmoid(decay_logit.astype(jnp.float32))
    y_fwd, s_fwd = retention_scan(q, k, v, log_gamma[0], s0[:, 0])
    y_bwd, s_bwd = retention_scan(q[:, ::-1], k[:, ::-1], v[:, ::-1], log_gamma[1], s0[:, 1])
    y = y_fwd + y_bwd[:, ::-1]
    mu = jnp.mean(y, axis=-1, keepdims=True)
    var = jnp.mean(jnp.square(y - mu), axis=-1, keepdims=True)
    y = ((y - mu) * lax.rsqrt(var + EPS)).reshape(b, l, RET_V_WIDTH) * norm_w.astype(jnp.float32)
    y = jax.nn.silu(r_g) * y.astype(r_g.dtype)
    return y, jnp.stack([s_fwd, s_bwd], axis=1).astype(s0.dtype)


def chunk_mlp(u, v, norm_w, w_spatial, b_spatial):
    b, l, _ = v.shape
    n = l // CHUNK
    v = rms_norm(v, norm_w).reshape(b, n, CHUNK, CM_GROUPS, CM_GROUP_CH)
    mixed = jnp.einsum('gts,bnsgc->bntgc', w_spatial, v) + b_spatial.T[None, None, :, :, None]
    return u * mixed.reshape(b, l, CM_WIDTH)


def mla_kv(c_kv, k_rope, w_ukv, k_norm_w):
    b, l, _ = c_kv.shape
    kv = (c_kv @ w_ukv).reshape(b, l, MLA_HEADS, MLA_NOPE + MLA_V)
    k_nope, v = jnp.split(kv, [MLA_NOPE], axis=-1)
    k = jnp.concatenate([k_nope, jnp.broadcast_to(k_rope[:, :, None, :], (b, l, MLA_HEADS, MLA_ROPE))], axis=-1)
    return rms_norm(k, k_norm_w), v


def token_mixer(h, w, ctx):
    b, l, _ = h.shape
    split_points = np.cumsum(IN_SIZES)[:-1].tolist()
    cm_u, cm_v, r_q, r_k, r_v, r_g, m_cq, m_ckv, m_kr, gate_logits = jnp.split(h @ w['w_in'], split_points, axis=-1)

    y_a = chunk_mlp(jax.nn.gelu(cm_u), jax.nn.gelu(cm_v), w['cm_norm_w'], w['cm_w_spatial'], w['cm_b_spatial'])

    s0 = jnp.zeros((b, 2, RET_HEADS, RET_DK, RET_DV), h.dtype) if ctx is None else ctx[2]
    y_b, ret_state = retention_branch(r_q, r_k, r_v, r_g, w['ret_decay_logit'], w['ret_norm_w'], s0)

    c_q = rms_norm(m_cq, w['mla_q_norm_w'])
    c_kv = rms_norm(m_ckv, w['mla_kv_norm_w'])
    q = rms_norm((c_q @ w['mla_w_uq']).reshape(b, l, MLA_HEADS, MLA_QK_DIM), w['qk_norm_q_w'])
    k, v = mla_kv(c_kv, m_kr, w['mla_w_ukv'], w['qk_norm_k_w'])
    if ctx is None:
        attn = block_attention(q, k, v)
        new_ctx = (c_kv, m_kr, ret_state)
    else:
        cos, sin = axial_rope_tables(l)
        cos, sin = cos[:, None, :], sin[:, None, :]
        q = rope_tail(q, cos, sin)
        k = rope_tail(k, cos, sin)
        k_ctx, v_ctx = mla_kv(ctx[0], ctx[1], w['mla_w_ukv'], w['qk_norm_k_w'])
        attn = block_attention(q, jnp.concatenate([k, k_ctx], axis=1), jnp.concatenate([v, v_ctx], axis=1))
        new_ctx = None
    y_c = attn.reshape(b, l, MLA_WIDTH)

    g_a, g_b, g_c = jnp.split(jax.nn.sigmoid(gate_logits), N_BRANCHES, axis=-1)
    merged = (g_a * (y_a @ w['w_branch'][0]) + g_b * (y_b @ w['w_branch'][1])
              + g_c * (y_c @ w['w_branch'][2]))
    return merged @ w['w_out'], new_ctx


def trunk_layer(x, cond, w, ctx):
    mod = (jax.nn.silu(cond) @ w['w_mod'] + w['b_mod'])[:, None, :]
    sh1, sc1, g1, sh2, sc2, g2, sh3, sc3, g3 = jnp.split(mod, N_MOD, axis=-1)
    x = x + 0.5 * g1 * swiglu(modulate(rms_norm(x), sh1, sc1), w['ffn1_w_gate_up'], w['ffn1_w_down'])
    mix, new_ctx = token_mixer(modulate(rms_norm(x), sh2, sc2), w, ctx)
    x = x + g2 * mix
    x = x + 0.5 * g3 * swiglu(modulate(rms_norm(x), sh3, sc3), w['ffn2_w_gate_up'], w['ffn2_w_down'])
    return x, new_ctx


def setup_inputs(seed: int = 0) -> dict:
    key = jax.random.key(seed)
    ks = iter(jax.random.split(key, 32))
    f32 = jnp.float32

    def normal(shape, scale):
        return jax.random.normal(next(ks), shape, f32) * scale

    decay_base = jnp.log(jnp.power(2.0, RET_DECAY_EXP0 + jnp.arange(RET_HEADS, dtype=f32)) - 1.0)
    return {
        'x_prompt': normal((BATCH, SEQ, D_MODEL), 1.0),
        'x_sample': normal((DEC_BATCH, DEC_SEQ, D_MODEL), 1.0),
        'cache_mla_ckv': normal((DEC_BATCH, DEPTH, PAST_LEN, MLA_KV_RANK), 1.0),
        'cache_mla_krope': normal((DEC_BATCH, DEPTH, PAST_LEN, MLA_ROPE), 1.0),
        'state_retention': normal((DEC_BATCH, DEPTH, 2, RET_HEADS, RET_DK, RET_DV), 1.0),
        'c': normal((DEC_BATCH, D_MODEL), 1.0),
        'c_ctx': normal((D_MODEL,), 1.0),
        'w_mod': normal((DEPTH, D_MODEL, N_MOD * D_MODEL), 0.5 * D_MODEL ** -0.5),
        'b_mod': normal((DEPTH, N_MOD * D_MODEL), 0.02),
        'ffn1_w_gate_up': normal((DEPTH, D_MODEL, 2 * D_FF), D_MODEL ** -0.5),
        'ffn1_w_down': normal((DEPTH, D_FF, D_MODEL), D_FF ** -0.5),
        'ffn2_w_gate_up': normal((DEPTH, D_MODEL, 2 * D_FF), D_MODEL ** -0.5),
        'ffn2_w_down': normal((DEPTH, D_FF, D_MODEL), D_FF ** -0.5),
        'w_in': normal((DEPTH, D_MODEL, IN_WIDTH), D_MODEL ** -0.5),
        'cm_norm_w': 1.0 + normal((DEPTH, CM_WIDTH), 0.1),
        'cm_w_spatial': normal((DEPTH, CM_GROUPS, CHUNK, CHUNK), CHUNK ** -0.5),
        'cm_b_spatial': 1.0 + normal((DEPTH, CM_GROUPS, CHUNK), 0.1),
        'ret_decay_logit': decay_base[None, None, :] + normal((DEPTH, 2, RET_HEADS), 0.1),
        'ret_norm_w': 1.0 + normal((DEPTH, RET_V_WIDTH), 0.1),
        'mla_q_norm_w': 1.0 + normal((DEPTH, MLA_Q_RANK), 0.1),
        'mla_kv_norm_w': 1.0 + normal((DEPTH, MLA_KV_RANK), 0.1),
        'mla_w_uq': normal((DEPTH, MLA_Q_RANK, MLA_HEADS * MLA_QK_DIM), MLA_Q_RANK ** -0.5),
        'mla_w_ukv': normal((DEPTH, MLA_KV_RANK, MLA_HEADS * (MLA_NOPE + MLA_V)), MLA_KV_RANK ** -0.5),
        'qk_norm_q_w': 1.0 + normal((DEPTH, MLA_QK_DIM), 0.1),
        'qk_norm_k_w': 1.0 + normal((DEPTH, MLA_QK_DIM), 0.1),
        'w_branch': normal((DEPTH, N_BRANCHES, BRANCH_WIDTH, D_MODEL), BRANCH_WIDTH ** -0.5),
        'w_out': normal((DEPTH, D_MODEL, D_MODEL), D_MODEL ** -0.5),
    }


def reference(x_prompt, x_sample, cache_mla_ckv, cache_mla_krope, state_retention, c, c_ctx,
              w_mod, b_mod, ffn1_w_gate_up, ffn1_w_down, ffn2_w_gate_up, ffn2_w_down, w_in,
              cm_norm_w, cm_w_spatial, cm_b_spatial, ret_decay_logit, ret_norm_w,
              mla_q_norm_w, mla_kv_norm_w, mla_w_uq, mla_w_ukv, qk_norm_q_w, qk_norm_k_w,
              w_branch, w_out):
    cond_ctx = c_ctx[None, :]
    y_prompt = x_prompt
    y_sample = x_sample
    ckv_layers, krope_layers, ret_layers = [], [], []
    for l in range(DEPTH):
        w = {
            'w_mod': w_mod[l], 'b_mod': b_mod[l],
            'ffn1_w_gate_up': ffn1_w_gate_up[l], 'ffn1_w_down': ffn1_w_down[l],
            'ffn2_w_gate_up': ffn2_w_gate_up[l], 'ffn2_w_down': ffn2_w_down[l],
            'w_in': w_in[l],
            'cm_norm_w': cm_norm_w[l], 'cm_w_spatial': cm_w_spatial[l], 'cm_b_spatial': cm_b_spatial[l],
            'ret_decay_logit': ret_decay_logit[l], 'ret_norm_w': ret_norm_w[l],
            'mla_q_norm_w': mla_q_norm_w[l], 'mla_kv_norm_w': mla_kv_norm_w[l],
            'mla_w_uq': mla_w_uq[l], 'mla_w_ukv': mla_w_ukv[l],
            'qk_norm_q_w': qk_norm_q_w[l], 'qk_norm_k_w': qk_norm_k_w[l],
            'w_branch': w_branch[l], 'w_out': w_out[l],
        }
        y_prompt, (ckv_l, krope_l, ret_l) = trunk_layer(y_prompt, cond_ctx, w, None)
        ckv_layers.append(ckv_l)
        krope_layers.append(krope_l)
        ret_layers.append(ret_l)
        y_sample, _ = trunk_layer(y_sample, c, w,
                                  (cache_mla_ckv[:, l], cache_mla_krope[:, l], state_retention[:, l]))
    new_mla_ckv = jnp.stack(ckv_layers, axis=1)
    new_mla_krope = jnp.stack(krope_layers, axis=1)
    new_ret_state = jnp.stack(ret_layers, axis=1)
    return (y_prompt, y_sample, new_mla_ckv, new_mla_krope, new_ret_state)
```

```python
import functools

import jax
import jax.numpy as jnp
from jax import lax
from jax.experimental import pallas as pl
from jax.experimental.pallas import tpu as pltpu

F32 = jnp.float32
BF16 = jnp.bfloat16

D_MODEL = 2048
BATCH = 16
SEQ = 256
DEPTH = 4
DEC_BATCH = 8
DEC_SEQ = 2048
PAST_LEN = 512
GRID_W = 64
CHUNK = 128
D_FF = 5632
N_MOD = 9
EPS = 1e-6
ROPE_THETA = 10000.0
CM_GROUPS = 8
CM_WIDTH = 1024
RET_HEADS = 4
RET_DK = 128
RET_DV = 256
RET_QK_WIDTH = RET_HEADS * RET_DK
RET_V_WIDTH = RET_HEADS * RET_DV
MLA_HEADS = 8
MLA_NOPE = 128
MLA_ROPE = 64
MLA_V = 128
MLA_Q_RANK = 512
MLA_KV_RANK = 512
MLA_QK_DIM = MLA_NOPE + MLA_ROPE
MLA_SLAB = 256
BRANCH_WIDTH = 1024

T_P = BATCH * SEQ
T_S = DEC_BATCH * DEC_SEQ
T = T_P + T_S
MOD_ROWS = 16

MIB = 1024 * 1024


def _cparams(semantics, vmem_mib):
    return pltpu.CompilerParams(dimension_semantics=semantics, vmem_limit_bytes=vmem_mib * MIB)


def _mod_row(i, tm):
    n_p = T_P // tm
    return jnp.where(i < n_p, 0, 1 + (i - n_p) // (DEC_SEQ // tm))


def _mod_spec(layer, chunk, tm):
    return pl.BlockSpec((None, None, None, 1, D_MODEL),
                        lambda i, *_: (layer, _mod_row(i, tm), chunk, 0, 0))


def _rms(x):
    return x * lax.rsqrt(jnp.mean(x * x, axis=-1, keepdims=True) + EPS)


def _norm_mod(x, sh, sc):
    return _rms(x) * (1 + sc) + sh


def _dot(a, b):
    return jnp.dot(a, b, preferred_element_type=F32)


def _dot_nt(a, b):
    return lax.dot_general(a, b, (((1,), (1,)), ((), ())), preferred_element_type=F32)


def _dot_tn(a, b):
    return lax.dot_general(a, b, (((0,), (0,)), ((), ())), preferred_element_type=F32)


def _mod_kernel(c_ref, w_ref, b_ref, o_ref):
    s = jax.nn.silu(c_ref[...]).astype(BF16)
    o_ref[...] = _dot(s, w_ref[...].astype(BF16)) + b_ref[...]


def _mod_call(cond, w_mod, b_mod):
    tn = 1024
    n = N_MOD * D_MODEL
    return pl.pallas_call(
        _mod_kernel,
        out_shape=jax.ShapeDtypeStruct((DEPTH, MOD_ROWS, n), F32),
        grid=(DEPTH, n // tn),
        in_specs=[
            pl.BlockSpec((MOD_ROWS, D_MODEL), lambda l, j: (0, 0)),
            pl.BlockSpec((None, D_MODEL, tn), lambda l, j: (l, 0, j)),
            pl.BlockSpec((None, 1, tn), lambda l, j: (l, 0, j)),
        ],
        out_specs=pl.BlockSpec((None, MOD_ROWS, tn), lambda l, j: (l, 0, j)),
        compiler_params=_cparams(("parallel", "parallel"), 40),
        name="mod",
    )(cond, w_mod, b_mod.reshape(DEPTH, 1, n))


def _prenorm_kernel(x_ref, sh_ref, sc_ref, h_ref):
    h_ref[...] = _norm_mod(x_ref[...], sh_ref[...], sc_ref[...]).astype(BF16)


def _prenorm_call(x, mod5):
    tm = 512
    return pl.pallas_call(
        _prenorm_kernel,
        out_shape=jax.ShapeDtypeStruct((T, D_MODEL), BF16),
        grid=(T // tm,),
        in_specs=[
            pl.BlockSpec((tm, D_MODEL), lambda i: (i, 0)),
            _mod_spec(0, 0, tm),
            _mod_spec(0, 1, tm),
        ],
        out_specs=pl.BlockSpec((tm, D_MODEL), lambda i: (i, 0)),
        compiler_params=_cparams(("parallel",), 32),
        name="prenorm",
    )(x, mod5, mod5)


def _ffn_kernel(h_ref, x_ref, wg_ref, wu_ref, wd_ref, g_ref, sh_ref, sc_ref, xo_ref, ho_ref, acc_ref):
    j = pl.program_id(1)
    h = h_ref[...]
    a = (jax.nn.silu(_dot(h, wg_ref[...])) * _dot(h, wu_ref[...])).astype(BF16)
    d = _dot(a, wd_ref[...])

    @pl.when(j == 0)
    def _():
        acc_ref[...] = d

    @pl.when(j > 0)
    def _():
        acc_ref[...] += d

    @pl.when(j == pl.num_programs(1) - 1)
    def _():
        xn = x_ref[...] + (0.5 * g_ref[...]) * acc_ref[...]
        xo_ref[...] = xn
        ho_ref[...] = _norm_mod(xn, sh_ref[...], sc_ref[...]).astype(BF16)


def _ffn_call(h, x, w_gu, w_d, mod5, layer, gate_chunk, next_layer, next_chunk):
    tm, tf = 512, 512
    nf = D_FF // tf
    return pl.pallas_call(
        _ffn_kernel,
        out_shape=(jax.ShapeDtypeStruct((T, D_MODEL), F32), jax.ShapeDtypeStruct((T, D_MODEL), BF16)),
        grid=(T // tm, nf),
        in_specs=[
            pl.BlockSpec((tm, D_MODEL), lambda i, j: (i, 0)),
            pl.BlockSpec((tm, D_MODEL), lambda i, j: (i, 0)),
            pl.BlockSpec((None, D_MODEL, tf), lambda i, j: (layer, 0, j)),
            pl.BlockSpec((None, D_MODEL, tf), lambda i, j: (layer, 0, j + nf)),
            pl.BlockSpec((None, tf, D_MODEL), lambda i, j: (layer, j, 0)),
            _mod_spec(layer, gate_chunk, tm),
            _mod_spec(next_layer, next_chunk, tm),
            _mod_spec(next_layer, next_chunk + 1, tm),
        ],
        out_specs=(pl.BlockSpec((tm, D_MODEL), lambda i, j: (i, 0)),
                   pl.BlockSpec((tm, D_MODEL), lambda i, j: (i, 0))),
        scratch_shapes=[pltpu.VMEM((tm, D_MODEL), F32)],
        compiler_params=_cparams(("parallel", "arbitrary"), 52),
        name="ffn",
    )(h, x, w_gu, w_gu, w_d, mod5, mod5, mod5)


def _cm_kernel(h_ref, w_ref, nw_ref, wsp_ref, bsp_ref, y_ref):
    p = _dot(h_ref[...], w_ref[...])
    u = jax.nn.gelu(p[:, :CM_WIDTH])
    v = jax.nn.gelu(p[:, CM_WIDTH:])
    vn = (_rms(v) * nw_ref[...]).astype(BF16)
    for c in range(h_ref.shape[0] // CHUNK):
        rows = slice(c * CHUNK, (c + 1) * CHUNK)
        for g in range(CM_GROUPS):
            cols = slice(g * CHUNK, (g + 1) * CHUNK)
            mixed = _dot(wsp_ref[g], vn[rows, cols]) + bsp_ref[g]
            y_ref[rows, cols] = (u[rows, cols] * mixed).astype(BF16)


def _cm_call(h, w_cm, cm_nw, wsp, bsp, layer):
    tm = 512
    return pl.pallas_call(
        _cm_kernel,
        out_shape=jax.ShapeDtypeStruct((T, CM_WIDTH), BF16),
        grid=(T // tm,),
        in_specs=[
            pl.BlockSpec((tm, D_MODEL), lambda i: (i, 0)),
            pl.BlockSpec((None, D_MODEL, 2 * CM_WIDTH), lambda i: (layer, 0, 0)),
            pl.BlockSpec((None, 1, CM_WIDTH), lambda i: (layer, 0, 0)),
            pl.BlockSpec((None, CM_GROUPS, CHUNK, CHUNK), lambda i: (layer, 0, 0, 0)),
            pl.BlockSpec((None, CM_GROUPS, CHUNK, CHUNK), lambda i: (layer, 0, 0, 0)),
        ],
        out_specs=pl.BlockSpec((tm, CM_WIDTH), lambda i: (i, 0)),
        compiler_params=_cparams(("parallel",), 48),
        name="chunk_mlp",
    )(h, w_cm, cm_nw, wsp, bsp)


def _retproj_kernel(h_ref, w_ref, s_ref, o_ref):
    o_ref[...] = (_dot(h_ref[...], w_ref[...]) * s_ref[...]).astype(BF16)


def _retproj_call(h, w_ret, col_scale, layer):
    tm, tn = 1024, 1024
    n = 2 * RET_QK_WIDTH + 2 * RET_V_WIDTH
    return pl.pallas_call(
        _retproj_kernel,
        out_shape=jax.ShapeDtypeStruct((T, n), BF16),
        grid=(T // tm, n // tn),
        in_specs=[
            pl.BlockSpec((tm, D_MODEL), lambda i, j: (i, 0)),
            pl.BlockSpec((None, D_MODEL, tn), lambda i, j: (layer, 0, j)),
            pl.BlockSpec((1, tn), lambda i, j: (0, j)),
        ],
        out_specs=pl.BlockSpec((tm, tn), lambda i, j: (i, j)),
        compiler_params=_cparams(("parallel", "parallel"), 40),
        name="ret_proj",
    )(h, w_ret, col_scale)


def _ret_kernel(*refs, layer, n_chunks, has_s0, emit_state):
    refs = list(refs)
    lg_ref, q_ref, k_ref, v_ref, g_ref, nw_ref = refs[:6]
    pos = 6
    s0_ref = None
    if has_s0:
        s0_ref = refs[pos]
        pos += 1
    pos += 1
    y_ref = refs[pos]
    pos += 1
    so_ref = None
    if emit_state:
        so_ref = refs[pos]
        pos += 1
    yacc_ref = refs[pos]

    head = pl.program_id(1)
    row = lax.broadcasted_iota(jnp.int32, (CHUNK, CHUNK), 0).astype(F32)
    col = lax.broadcasted_iota(jnp.int32, (CHUNK, CHUNK), 1).astype(F32)
    for d in range(2):
        lg = lg_ref[layer * 2 * RET_HEADS + d * RET_HEADS + head]
        if d == 0:
            rel = row - col
            q_decay = jnp.exp(lg * (row + 1.0))
            k_decay = jnp.exp(lg * (CHUNK - 1.0 - row))
        else:
            rel = col - row
            q_decay = jnp.exp(lg * (CHUNK - row))
            k_decay = jnp.exp(lg * row)
        intra = jnp.where(rel >= 0, jnp.exp(lg * jnp.maximum(rel, 0.0)), 0.0)
        chunk_decay = jnp.exp(jnp.full((RET_DK, RET_DV), lg * CHUNK, F32))

        def body(t, state, d=d, intra=intra, q_decay=q_decay, k_decay=k_decay, chunk_decay=chunk_decay):
            c = t if d == 0 else n_chunks - 1 - t
            rows = pl.ds(pl.multiple_of(c * CHUNK, CHUNK), CHUNK)
            qc = q_ref[rows, :]
            kc = k_ref[rows, :]
            vc = v_ref[rows, :]
            scores = (_dot_nt(qc, kc) * intra).astype(BF16)
            qd = (qc.astype(F32) * q_decay).astype(BF16)
            kd = (kc.astype(F32) * k_decay).astype(BF16)
            out = _dot(scores, vc) + _dot(qd, state.astype(BF16))
            if d == 0:
                yacc_ref[rows, :] = out
            else:
                yacc_ref[rows, :] += out
            return chunk_decay * state + _dot_tn(kd, vc)

        s_init = s0_ref[d] if has_s0 else jnp.zeros((RET_DK, RET_DV), F32)
        s_final = lax.fori_loop(0, n_chunks, body, s_init)
        if emit_state:
            so_ref[d] = s_final

    y = yacc_ref[...]
    mu = jnp.mean(y, axis=-1, keepdims=True)
    var = jnp.mean(jnp.square(y - mu), axis=-1, keepdims=True)
    yn = (y - mu) * lax.rsqrt(var + EPS) * nw_ref[...]
    y_ref[...] = (jax.nn.silu(g_ref[...].astype(F32)) * yn).astype(BF16)


def _ret_call(lg, rp, ret_nw, s0, y_prev, layer, *, prompt):
    seq = SEQ if prompt else DEC_SEQ
    nb = BATCH if prompt else DEC_BATCH
    rb0 = 0 if prompt else T_P // DEC_SEQ
    kq = RET_QK_WIDTH // RET_DK
    kv = 2 * RET_QK_WIDTH // RET_DV
    kg = kv + RET_HEADS
    in_specs = [
        pl.BlockSpec(memory_space=pltpu.SMEM),
        pl.BlockSpec((seq, RET_DK), lambda b, h: (rb0 + b, h)),
        pl.BlockSpec((seq, RET_DK), lambda b, h: (rb0 + b, kq + h)),
        pl.BlockSpec((seq, RET_DV), lambda b, h: (rb0 + b, kv + h)),
        pl.BlockSpec((seq, RET_DV), lambda b, h: (rb0 + b, kg + h)),
        pl.BlockSpec((None, 1, RET_DV), lambda b, h: (layer, 0, h)),
    ]
    args = [lg, rp, rp, rp, rp, ret_nw]
    if not prompt:
        in_specs.append(pl.BlockSpec((None, None, 2, None, RET_DK, RET_DV), lambda b, h: (b, layer, 0, h, 0, 0)))
        args.append(s0)
    in_specs.append(pl.BlockSpec(memory_space=pl.ANY))
    args.append(y_prev)
    out_shape = [jax.ShapeDtypeStruct((T, RET_V_WIDTH), BF16)]
    out_specs = [pl.BlockSpec((seq, RET_DV), lambda b, h: (rb0 + b, h))]
    if prompt:
        out_shape.append(jax.ShapeDtypeStruct((BATCH, 2, RET_HEADS, RET_DK, RET_DV), F32))
        out_specs.append(pl.BlockSpec((None, 2, None, RET_DK, RET_DV), lambda b, h: (b, 0, h, 0, 0)))
    return pl.pallas_call(
        functools.partial(_ret_kernel, layer=layer, n_chunks=seq // CHUNK, has_s0=not prompt,
                          emit_state=prompt),
        out_shape=tuple(out_shape),
        grid=(nb, RET_HEADS),
        in_specs=in_specs,
        out_specs=tuple(out_specs),
        scratch_shapes=[pltpu.VMEM((seq, RET_DV), F32)],
        input_output_aliases={len(args) - 1: 0},
        compiler_params=_cparams(("parallel", "parallel"), 32),
        name="retention_prompt" if prompt else "retention_sample",
    )(*args)


def _rope_lanes(x, cos_t, sin_t):
    return x * cos_t + pltpu.roll(x, 64, 1) * sin_t


def _head_keys(kvf, kr2, rotk, wkn, k_ref, v_ref):
    lane = lax.broadcasted_iota(jnp.int32, (1, 2 * MLA_ROPE), 1)
    ssq_kr = jnp.sum(jnp.where(lane < MLA_ROPE, kr2 * kr2, 0.0), axis=-1, keepdims=True)
    nk = MLA_HEADS * MLA_NOPE
    for hh in range(MLA_HEADS):
        kn = kvf[:, hh * MLA_NOPE:(hh + 1) * MLA_NOPE]
        ssq = jnp.sum(kn * kn, axis=-1, keepdims=True) + ssq_kr
        r = lax.rsqrt(ssq / MLA_QK_DIM + EPS)
        k_ref[hh, :, 0:MLA_NOPE] = (kn * r * wkn).astype(BF16)
        k_ref[hh, :, MLA_NOPE:MLA_SLAB] = (rotk * r).astype(BF16)
        v_ref[hh] = kvf[:, nk + hh * MLA_V: nk + (hh + 1) * MLA_V].astype(BF16)


def _mla_kernel(h_ref, w1_ref, wuq_ref, wukv_ref, qnw_ref, kvnw_ref, wq_ref, wkn_ref, wkr_ref, cos_ref, sin_ref,
                q_ref, k_ref, v_ref, ckv_ref, kr_ref):
    p = _dot(h_ref[...], w1_ref[...])
    cq = _rms(p[:, :MLA_Q_RANK]) * qnw_ref[...]
    ckv = _rms(p[:, MLA_Q_RANK:MLA_Q_RANK + MLA_KV_RANK]) * kvnw_ref[...]
    kr2 = p[:, MLA_Q_RANK + MLA_KV_RANK:]
    ckv_ref[...] = ckv
    kr_ref[...] = kr2[:, :MLA_ROPE]
    cos_t = cos_ref[...]
    sin_t = sin_ref[...]
    qf = _dot(cq.astype(BF16), wuq_ref[...])
    kvf = _dot(ckv.astype(BF16), wukv_ref[...])
    lane = lax.broadcasted_iota(jnp.int32, (1, MLA_SLAB), 1)
    for hh in range(MLA_HEADS):
        slab = qf[:, hh * MLA_SLAB:(hh + 1) * MLA_SLAB]
        ssq = jnp.sum(jnp.where(lane < MLA_QK_DIM, slab * slab, 0.0), axis=-1, keepdims=True)
        n = slab * lax.rsqrt(ssq / MLA_QK_DIM + EPS) * wq_ref[...]
        q_ref[hh, :, 0:MLA_NOPE] = n[:, :MLA_NOPE].astype(BF16)
        q_ref[hh, :, MLA_NOPE:MLA_SLAB] = _rope_lanes(n[:, MLA_NOPE:], cos_t, sin_t).astype(BF16)
    rotk = _rope_lanes(kr2 * wkr_ref[...], cos_t, sin_t)
    _head_keys(kvf, kr2, rotk, wkn_ref[...], k_ref, v_ref)


def _mla_call(h, w1, wuq, wukv, qnw, kvnw, wq, wkn, wkr, cos_t, sin_t, layer):
    tm = 512
    n1 = MLA_Q_RANK + MLA_KV_RANK + 2 * MLA_ROPE
    lsel = lambda i: (layer, 0, 0)
    return pl.pallas_call(
        _mla_kernel,
        out_shape=(
            jax.ShapeDtypeStruct((MLA_HEADS, T, MLA_SLAB), BF16),
            jax.ShapeDtypeStruct((MLA_HEADS, T, MLA_SLAB), BF16),
            jax.ShapeDtypeStruct((MLA_HEADS, T, MLA_V), BF16),
            jax.ShapeDtypeStruct((T, MLA_KV_RANK), F32),
            jax.ShapeDtypeStruct((T, MLA_ROPE), F32),
        ),
        grid=(T // tm,),
        in_specs=[
            pl.BlockSpec((tm, D_MODEL), lambda i: (i, 0)),
            pl.BlockSpec((None, D_MODEL, n1), lsel),
            pl.BlockSpec((None, MLA_Q_RANK, MLA_HEADS * MLA_SLAB), lsel),
            pl.BlockSpec((None, MLA_KV_RANK, MLA_HEADS * (MLA_NOPE + MLA_V)), lsel),
            pl.BlockSpec((None, 1, MLA_Q_RANK), lsel),
            pl.BlockSpec((None, 1, MLA_KV_RANK), lsel),
            pl.BlockSpec((None, 1, MLA_SLAB), lsel),
            pl.BlockSpec((None, 1, MLA_NOPE), lsel),
            pl.BlockSpec((None, 1, 2 * MLA_ROPE), lsel),
            pl.BlockSpec((tm, 2 * MLA_ROPE), lambda i: (i, 0)),
            pl.BlockSpec((tm, 2 * MLA_ROPE), lambda i: (i, 0)),
        ],
        out_specs=(
            pl.BlockSpec((MLA_HEADS, tm, MLA_SLAB), lambda i: (0, i, 0)),
            pl.BlockSpec((MLA_HEADS, tm, MLA_SLAB), lambda i: (0, i, 0)),
            pl.BlockSpec((MLA_HEADS, tm, MLA_V), lambda i: (0, i, 0)),
            pl.BlockSpec((tm, MLA_KV_RANK), lambda i: (i, 0)),
            pl.BlockSpec((tm, MLA_ROPE), lambda i: (i, 0)),
        ),
        compiler_params=_cparams(("parallel",), 48),
        name="mla_proj",
    )(h, w1, wuq, wukv, qnw, kvnw, wq, wkn, wkr, cos_t, sin_t)


def _ctxkv_kernel(c_ref, kr_ref, wukv_ref, wkn_ref, wkr_ref, k_ref, v_ref):
    kvf = _dot(c_ref[...].astype(BF16), wukv_ref[...])
    kr2 = kr_ref[...]
    _head_keys(kvf, kr2, kr2 * wkr_ref[...], wkn_ref[...], k_ref, v_ref)


def _ctxkv_call(cache_ckv, cache_kr_pad, wukv, wkn, wkr):
    lsel = lambda l, b: (l, 0, 0)
    return pl.pallas_call(
        _ctxkv_kernel,
        out_shape=(
            jax.ShapeDtypeStruct((DEPTH, MLA_HEADS, DEC_BATCH * PAST_LEN, MLA_SLAB), BF16),
            jax.ShapeDtypeStruct((DEPTH, MLA_HEADS, DEC_BATCH * PAST_LEN, MLA_V), BF16),
        ),
        grid=(DEPTH, DEC_BATCH),
        in_specs=[
            pl.BlockSpec((None, None, PAST_LEN, MLA_KV_RANK), lambda l, b: (b, l, 0, 0)),
            pl.BlockSpec((None, None, PAST_LEN, 2 * MLA_ROPE), lambda l, b: (b, l, 0, 0)),
            pl.BlockSpec((None, MLA_KV_RANK, MLA_HEADS * (MLA_NOPE + MLA_V)), lsel),
            pl.BlockSpec((None, 1, MLA_NOPE), lsel),
            pl.BlockSpec((None, 1, 2 * MLA_ROPE), lsel),
        ],
        out_specs=(
            pl.BlockSpec((None, MLA_HEADS, PAST_LEN, MLA_SLAB), lambda l, b: (l, 0, b, 0)),
            pl.BlockSpec((None, MLA_HEADS, PAST_LEN, MLA_V), lambda l, b: (l, 0, b, 0)),
        ),
        compiler_params=_cparams(("parallel", "parallel"), 32),
        name="ctx_kv",
    )(cache_ckv, cache_kr_pad, wukv, wkn, wkr)


def _attn_kernel(*refs, has_ctx):
    if has_ctx:
        q_ref, ks_ref, vs_ref, kc_ref, vc_ref, _, o_ref = refs
    else:
        q_ref, ks_ref, vs_ref, _, o_ref = refs
    q = q_ref[...]
    s1 = _dot_nt(q, ks_ref[...])
    m = jnp.max(s1, axis=-1, keepdims=True)
    if has_ctx:
        s2 = _dot_nt(q, kc_ref[...])
        m = jnp.maximum(m, jnp.max(s2, axis=-1, keepdims=True))
    p1 = jnp.exp(s1 - m)
    denom = jnp.sum(p1, axis=-1, keepdims=True)
    o = _dot(p1.astype(BF16), vs_ref[...])
    if has_ctx:
        p2 = jnp.exp(s2 - m)
        denom = denom + jnp.sum(p2, axis=-1, keepdims=True)
        o = o + _dot(p2.astype(BF16), vc_ref[...])
    o_ref[...] = (o / denom).astype(BF16)


def _attn_prompt_call(q, k, v, y_prev):
    return pl.pallas_call(
        functools.partial(_attn_kernel, has_ctx=False),
        out_shape=jax.ShapeDtypeStruct((T, MLA_HEADS * MLA_V), BF16),
        grid=(BATCH, MLA_HEADS),
        in_specs=[
            pl.BlockSpec((None, SEQ, MLA_SLAB), lambda b, h: (h, b, 0)),
            pl.BlockSpec((None, SEQ, MLA_SLAB), lambda b, h: (h, b, 0)),
            pl.BlockSpec((None, SEQ, MLA_V), lambda b, h: (h, b, 0)),
            pl.BlockSpec(memory_space=pl.ANY),
        ],
        out_specs=pl.BlockSpec((SEQ, MLA_V), lambda b, h: (b, h)),
        input_output_aliases={3: 0},
        compiler_params=_cparams(("parallel", "parallel"), 32),
        name="attn_prompt",
    )(q, k, v, y_prev)


def _attn_sample_call(q, k, v, kc, vc, y_prev, layer):
    tq = 512
    nq = DEC_SEQ // tq
    q0 = T_P // tq
    s0 = T_P // DEC_SEQ
    return pl.pallas_call(
        functools.partial(_attn_kernel, has_ctx=True),
        out_shape=jax.ShapeDtypeStruct((T, MLA_HEADS * MLA_V), BF16),
        grid=(DEC_BATCH, MLA_HEADS, nq),
        in_specs=[
            pl.BlockSpec((None, tq, MLA_SLAB), lambda b, h, i: (h, q0 + b * nq + i, 0)),
            pl.BlockSpec((None, DEC_SEQ, MLA_SLAB), lambda b, h, i: (h, s0 + b, 0)),
            pl.BlockSpec((None, DEC_SEQ, MLA_V), lambda b, h, i: (h, s0 + b, 0)),
            pl.BlockSpec((None, None, PAST_LEN, MLA_SLAB), lambda b, h, i: (layer, h, b, 0)),
            pl.BlockSpec((None, None, PAST_LEN, MLA_V), lambda b, h, i: (layer, h, b, 0)),
            pl.BlockSpec(memory_space=pl.ANY),
        ],
        out_specs=pl.BlockSpec((tq, MLA_V), lambda b, h, i: (q0 + b * nq + i, h)),
        input_output_aliases={5: 0},
        compiler_params=_cparams(("parallel", "parallel", "parallel"), 48),
        name="attn_sample",
    )(q, k, v, kc, vc, y_prev)


def _merge_kernel(h_ref, ya_ref, yb_ref, yc_ref, wga_ref, wgb_ref, wgc_ref, wb_ref, o_ref):
    h = h_ref[...]
    m = jax.nn.sigmoid(_dot(h, wga_ref[...])) * _dot(ya_ref[...], wb_ref[0])
    m = m + jax.nn.sigmoid(_dot(h, wgb_ref[...])) * _dot(yb_ref[...], wb_ref[1])
    m = m + jax.nn.sigmoid(_dot(h, wgc_ref[...])) * _dot(yc_ref[...], wb_ref[2])
    o_ref[...] = m.astype(BF16)


def _merge_call(h, ya, yb, yc, w_g, w_b, layer):
    tm, tn = 1024, 512
    nj = D_MODEL // tn
    yspec = pl.BlockSpec((tm, BRANCH_WIDTH), lambda i, j: (i, 0))
    return pl.pallas_call(
        _merge_kernel,
        out_shape=jax.ShapeDtypeStruct((T, D_MODEL), BF16),
        grid=(T // tm, nj),
        in_specs=[
            pl.BlockSpec((tm, D_MODEL), lambda i, j: (i, 0)),
            yspec, yspec, yspec,
            pl.BlockSpec((None, D_MODEL, tn), lambda i, j: (layer, 0, j)),
            pl.BlockSpec((None, D_MODEL, tn), lambda i, j: (layer, 0, j + nj)),
            pl.BlockSpec((None, D_MODEL, tn), lambda i, j: (layer, 0, j + 2 * nj)),
            pl.BlockSpec((None, 3, BRANCH_WIDTH, tn), lambda i, j: (layer, 0, 0, j)),
        ],
        out_specs=pl.BlockSpec((tm, tn), lambda i, j: (i, j)),
        compiler_params=_cparams(("parallel", "parallel"), 56),
        name="gated_merge",
    )(h, ya, yb, yc, w_g, w_g, w_g, w_b)


def _outproj_kernel(m_ref, w_ref, x_ref, g_ref, sh_ref, sc_ref, xo_ref, ho_ref):
    xn = x_ref[...] + g_ref[...] * _dot(m_ref[...], w_ref[...])
    xo_ref[...] = xn
    ho_ref[...] = _norm_mod(xn, sh_ref[...], sc_ref[...]).astype(BF16)


def _outproj_call(m, w_out, x, mod5, layer):
    tm = 512
    return pl.pallas_call(
        _outproj_kernel,
        out_shape=(jax.ShapeDtypeStruct((T, D_MODEL), F32), jax.ShapeDtypeStruct((T, D_MODEL), BF16)),
        grid=(T // tm,),
        in_specs=[
            pl.BlockSpec((tm, D_MODEL), lambda i: (i, 0)),
            pl.BlockSpec((None, D_MODEL, D_MODEL), lambda i: (layer, 0, 0)),
            pl.BlockSpec((tm, D_MODEL), lambda i: (i, 0)),
            _mod_spec(layer, 5, tm),
            _mod_spec(layer, 6, tm),
            _mod_spec(layer, 7, tm),
        ],
        out_specs=(pl.BlockSpec((tm, D_MODEL), lambda i: (i, 0)),
                   pl.BlockSpec((tm, D_MODEL), lambda i: (i, 0))),
        compiler_params=_cparams(("parallel",), 52),
        name="out_proj",
    )(m, w_out, x, mod5, mod5, mod5)


def _swap_halves(x):
    half = x.shape[-1] // 2
    return jnp.concatenate([x[..., half:], x[..., :half]], axis=-1)


def _rope_tables():
    t = jnp.arange(DEC_SEQ)
    row = (t // GRID_W).astype(F32)
    col = (t % GRID_W).astype(F32)
    n_freq = MLA_ROPE // 4
    freqs = jnp.power(ROPE_THETA, -jnp.arange(n_freq, dtype=F32) / n_freq)
    ang = jnp.concatenate([row[:, None] * freqs, col[:, None] * freqs], axis=-1)
    cos, sin = jnp.cos(ang), jnp.sin(ang)
    zeros = jnp.zeros((DEC_SEQ, MLA_ROPE), F32)
    cos_s = jnp.tile(jnp.concatenate([cos, cos, zeros], axis=-1), (DEC_BATCH, 1))
    sin_s = jnp.tile(jnp.concatenate([-sin, sin, zeros], axis=-1), (DEC_BATCH, 1))
    cos_p = jnp.concatenate([jnp.ones((T_P, MLA_ROPE), F32), jnp.zeros((T_P, MLA_ROPE), F32)], axis=-1)
    sin_p = jnp.zeros((T_P, 2 * MLA_ROPE), F32)
    return jnp.concatenate([cos_p, cos_s], axis=0), jnp.concatenate([sin_p, sin_s], axis=0)


def kernel(x_prompt, x_sample, cache_mla_ckv, cache_mla_krope, state_retention, c, c_ctx, w_mod, b_mod,
           ffn1_w_gate_up, ffn1_w_down, ffn2_w_gate_up, ffn2_w_down, w_in, cm_norm_w, cm_w_spatial,
           cm_b_spatial, ret_decay_logit, ret_norm_w, mla_q_norm_w, mla_kv_norm_w, mla_w_uq, mla_w_ukv,
           qk_norm_q_w, qk_norm_k_w, w_branch, w_out):
    o_cm, o_ret, o_mla, o_kr, o_gate = 0, 2 * CM_WIDTH, 2 * CM_WIDTH + 2 * RET_QK_WIDTH + 2 * RET_V_WIDTH, 0, 0
    o_kr = o_mla + MLA_Q_RANK + MLA_KV_RANK
    o_gate = o_kr + MLA_ROPE
    w_cm = w_in[:, :, o_cm:o_ret].astype(BF16)
    w_ret = w_in[:, :, o_ret:o_mla].astype(BF16)
    w_kr = w_in[:, :, o_kr:o_gate]
    w_mla1 = jnp.concatenate([w_in[:, :, o_mla:o_kr], w_kr, _swap_halves(w_kr)], axis=-1).astype(BF16)
    w_gate = w_in[:, :, o_gate:].astype(BF16)
    ffn1_gu = ffn1_w_gate_up.astype(BF16)
    ffn1_d = ffn1_w_down.astype(BF16)
    ffn2_gu = ffn2_w_gate_up.astype(BF16)
    ffn2_d = ffn2_w_down.astype(BF16)
    w_b = w_branch.astype(BF16)
    w_o = w_out.astype(BF16)

    uq = mla_w_uq.reshape(DEPTH, MLA_Q_RANK, MLA_HEADS, MLA_QK_DIM)
    uq_rope = uq[..., MLA_NOPE:]
    wuq = jnp.concatenate([uq[..., :MLA_NOPE], uq_rope, _swap_halves(uq_rope)], axis=-1)
    wuq = wuq.reshape(DEPTH, MLA_Q_RANK, MLA_HEADS * MLA_SLAB).astype(BF16)
    ukv = mla_w_ukv.reshape(DEPTH, MLA_KV_RANK, MLA_HEADS, MLA_NOPE + MLA_V)
    wukv = jnp.concatenate([ukv[..., :MLA_NOPE].reshape(DEPTH, MLA_KV_RANK, MLA_HEADS * MLA_NOPE),
                            ukv[..., MLA_NOPE:].reshape(DEPTH, MLA_KV_RANK, MLA_HEADS * MLA_V)],
                           axis=-1).astype(BF16)
    q_rope_w = qk_norm_q_w[:, MLA_NOPE:]
    wq = (jnp.concatenate([qk_norm_q_w[:, :MLA_NOPE], q_rope_w, _swap_halves(q_rope_w)], axis=-1)
          * (MLA_QK_DIM ** -0.5)).reshape(DEPTH, 1, MLA_SLAB)
    k_rope_w = qk_norm_k_w[:, MLA_NOPE:]
    wkn = qk_norm_k_w[:, :MLA_NOPE].reshape(DEPTH, 1, MLA_NOPE)
    wkr = jnp.concatenate([k_rope_w, _swap_halves(k_rope_w)], axis=-1).reshape(DEPTH, 1, 2 * MLA_ROPE)
    qnw = mla_q_norm_w.reshape(DEPTH, 1, MLA_Q_RANK)
    kvnw = mla_kv_norm_w.reshape(DEPTH, 1, MLA_KV_RANK)
    cm_nw = cm_norm_w.reshape(DEPTH, 1, CM_WIDTH)
    wsp = cm_w_spatial.astype(BF16)
    bsp = jnp.broadcast_to(cm_b_spatial[..., None], (DEPTH, CM_GROUPS, CHUNK, CHUNK))
    ret_nw = ret_norm_w.reshape(DEPTH, 1, RET_V_WIDTH)
    lg = jax.nn.log_sigmoid(ret_decay_logit.astype(F32)).reshape(-1)
    ret_scale = jnp.concatenate([jnp.ones((RET_QK_WIDTH,), F32), jnp.full((RET_QK_WIDTH,), RET_DK ** -0.5, F32),
                                 jnp.ones((2 * RET_V_WIDTH,), F32)]).reshape(1, -1)
    cos_t, sin_t = _rope_tables()
    cache_kr_pad = jnp.pad(cache_mla_krope, ((0, 0), (0, 0), (0, 0), (0, MLA_ROPE)))

    cond = jnp.concatenate([c_ctx[None, :], c, jnp.zeros((MOD_ROWS - 1 - DEC_BATCH, D_MODEL), F32)], axis=0)
    x = jnp.concatenate([x_prompt.reshape(T_P, D_MODEL), x_sample.reshape(T_S, D_MODEL)], axis=0)

    mod5 = _mod_call(cond, w_mod, b_mod).reshape(DEPTH, MOD_ROWS, N_MOD, 1, D_MODEL)
    kc, vc = _ctxkv_call(cache_mla_ckv, cache_kr_pad, wukv, wkn, wkr)
    h = _prenorm_call(x, mod5)

    ckv_layers, krope_layers, ret_layers = [], [], []
    for l in range(DEPTH):
        x, h = _ffn_call(h, x, ffn1_gu, ffn1_d, mod5, l, 2, l, 3)

        ya = _cm_call(h, w_cm, cm_nw, wsp, bsp, l)

        rp = _retproj_call(h, w_ret, ret_scale, l)
        yb_init = jnp.zeros((T, RET_V_WIDTH), BF16)
        yb, ret_state = _ret_call(lg, rp, ret_nw, None, yb_init, l, prompt=True)
        (yb,) = _ret_call(lg, rp, ret_nw, state_retention, yb, l, prompt=False)

        q, k, v, ckv, kr = _mla_call(h, w_mla1, wuq, wukv, qnw, kvnw, wq, wkn, wkr, cos_t, sin_t, l)
        yc_init = jnp.zeros((T, MLA_HEADS * MLA_V), BF16)
        yc = _attn_prompt_call(q, k, v, yc_init)
        yc = _attn_sample_call(q, k, v, kc, vc, yc, l)

        merged = _merge_call(h, ya, yb, yc, w_gate, w_b, l)
        x, h = _outproj_call(merged, w_o, x, mod5, l)

        nl = min(l + 1, DEPTH - 1)
        x, h = _ffn_call(h, x, ffn2_gu, ffn2_d, mod5, l, 8, nl, 0)

        ckv_layers.append(ckv[:T_P].reshape(BATCH, SEQ, MLA_KV_RANK))
        krope_layers.append(kr[:T_P].reshape(BATCH, SEQ, MLA_ROPE))
        ret_layers.append(ret_state)

    y_prompt = x[:T_P].reshape(BATCH, SEQ, D_MODEL)
    y_sample = x[T_P:].reshape(DEC_BATCH, DEC_SEQ, D_MODEL)
    return (y_prompt, y_sample, jnp.stack(ckv_layers, axis=1), jnp.stack(krope_layers, axis=1),
            jnp.stack(ret_layers, axis=1))
```

```python
import functools

import jax
import jax.numpy as jnp
from jax import lax
from jax.experimental import pallas as pl
from jax.experimental.pallas import tpu as pltpu

F32 = jnp.float32
BF16 = jnp.bfloat16

D_MODEL = 2048
BATCH = 16
SEQ = 256
DEPTH = 4
DEC_BATCH = 8
DEC_SEQ = 2048
PAST_LEN = 512
GRID_W = 64
CHUNK = 128
D_FF = 5632
N_MOD = 9
EPS = 1e-6
ROPE_THETA = 10000.0
CM_GROUPS = 8
CM_WIDTH = 1024
RET_HEADS = 4
RET_DK = 128
RET_DV = 256
RET_QK_WIDTH = RET_HEADS * RET_DK
RET_V_WIDTH = RET_HEADS * RET_DV
MLA_HEADS = 8
MLA_NOPE = 128
MLA_ROPE = 64
MLA_V = 128
MLA_Q_RANK = 512
MLA_KV_RANK = 512
MLA_QK_DIM = MLA_NOPE + MLA_ROPE
MLA_SLAB = 256
BRANCH_WIDTH = 1024

T_P = BATCH * SEQ
T_S = DEC_BATCH * DEC_SEQ
T = T_P + T_S
MOD_ROWS = 16

MIB = 1024 * 1024


def _cparams(semantics, vmem_mib):
    return pltpu.CompilerParams(dimension_semantics=semantics, vmem_limit_bytes=vmem_mib * MIB)


def _mod_row(i, tm):
    n_p = T_P // tm
    return jnp.where(i < n_p, 0, 1 + (i - n_p) // (DEC_SEQ // tm))


def _mod_spec(layer, chunk, tm):
    return pl.BlockSpec((None, None, None, 1, D_MODEL),
                        lambda i, *_: (layer, _mod_row(i, tm), chunk, 0, 0))


def _rms(x):
    return x * lax.rsqrt(jnp.mean(x * x, axis=-1, keepdims=True) + EPS)


def _norm_mod(x, sh, sc):
    return _rms(x) * (1 + sc) + sh


def _dot(a, b):
    return jnp.dot(a, b, preferred_element_type=F32)


def _dot_nt(a, b):
    return lax.dot_general(a, b, (((1,), (1,)), ((), ())), preferred_element_type=F32)


def _dot_tn(a, b):
    return lax.dot_general(a, b, (((0,), (0,)), ((), ())), preferred_element_type=F32)


def _mod_kernel(c_ref, w_ref, b_ref, o_ref):
    s = jax.nn.silu(c_ref[...]).astype(BF16)
    o_ref[...] = _dot(s, w_ref[...].astype(BF16)) + b_ref[...]


def _mod_call(cond, w_mod, b_mod):
    tn = 1024
    n = N_MOD * D_MODEL
    return pl.pallas_call(
        _mod_kernel,
        out_shape=jax.ShapeDtypeStruct((DEPTH, MOD_ROWS, n), F32),
        grid=(DEPTH, n // tn),
        in_specs=[
            pl.BlockSpec((MOD_ROWS, D_MODEL), lambda l, j: (0, 0)),
            pl.BlockSpec((None, D_MODEL, tn), lambda l, j: (l, 0, j)),
            pl.BlockSpec((None, 1, tn), lambda l, j: (l, 0, j)),
        ],
        out_specs=pl.BlockSpec((None, MOD_ROWS, tn), lambda l, j: (l, 0, j)),
        compiler_params=_cparams(("parallel", "parallel"), 40),
        name="mod",
    )(cond, w_mod, b_mod.reshape(DEPTH, 1, n))


def _prenorm_kernel(x_ref, sh_ref, sc_ref, h_ref):
    h_ref[...] = _norm_mod(x_ref[...], sh_ref[...], sc_ref[...]).astype(BF16)


def _prenorm_call(x, mod5):
    tm = 512
    return pl.pallas_call(
        _prenorm_kernel,
        out_shape=jax.ShapeDtypeStruct((T, D_MODEL), BF16),
        grid=(T // tm,),
        in_specs=[
            pl.BlockSpec((tm, D_MODEL), lambda i: (i, 0)),
            _mod_spec(0, 0, tm),
            _mod_spec(0, 1, tm),
        ],
        out_specs=pl.BlockSpec((tm, D_MODEL), lambda i: (i, 0)),
        compiler_params=_cparams(("parallel",), 32),
        name="prenorm",
    )(x, mod5, mod5)


def _ffn_up_kernel(h_ref, wg_ref, wu_ref, a_ref):
    h = h_ref[...]
    a_ref[...] = (jax.nn.silu(_dot(h, wg_ref[...])) * _dot(h, wu_ref[...])).astype(BF16)


def _ffn_down_kernel(a_ref, wd_ref, xs_ref, g_ref, sh_ref, sc_ref, xo_ref, ho_ref, *, nk):
    k = pl.program_id(1)
    tn = D_MODEL // nk
    half_g = 0.5 * g_ref[...]

    @pl.when(k == 0)
    def _():
        xo_ref[...] = jnp.zeros_like(xo_ref)

    for n in range(nk):
        cols = slice(n * tn, (n + 1) * tn)
        xo_ref[:, cols] += half_g[:, cols] * _dot(a_ref[...], wd_ref[:, cols])

    for n in range(nk):
        @pl.when(k == n)
        def _(n=n):
            xo_ref[:, n * tn:(n + 1) * tn] += xs_ref[...]

    @pl.when(k == nk - 1)
    def _():
        ho_ref[...] = _norm_mod(xo_ref[...], sh_ref[...], sc_ref[...]).astype(BF16)


def _ffn_call(h, x, w_gu, w_d, mod5, layer, gate_chunk, next_layer, next_chunk):
    tm, tf = 2048, 512
    nf = D_FF // tf
    a = pl.pallas_call(
        _ffn_up_kernel,
        out_shape=jax.ShapeDtypeStruct((T, D_FF), BF16),
        grid=(T // tm, nf),
        in_specs=[
            pl.BlockSpec((tm, D_MODEL), lambda i, j: (i, 0)),
            pl.BlockSpec((None, D_MODEL, tf), lambda i, j: (layer, 0, j)),
            pl.BlockSpec((None, D_MODEL, tf), lambda i, j: (layer, 0, j + nf)),
        ],
        out_specs=pl.BlockSpec((tm, tf), lambda i, j: (i, j)),
        compiler_params=_cparams(("parallel", "parallel"), 52),
        name="ffn_up",
    )(h, w_gu, w_gu)
    tm, nk = 1024, 4
    tk = D_FF // nk
    return pl.pallas_call(
        functools.partial(_ffn_down_kernel, nk=nk),
        out_shape=(jax.ShapeDtypeStruct((T, D_MODEL), F32), jax.ShapeDtypeStruct((T, D_MODEL), BF16)),
        grid=(T // tm, nk),
        in_specs=[
            pl.BlockSpec((tm, tk), lambda i, k: (i, k)),
            pl.BlockSpec((None, tk, D_MODEL), lambda i, k: (layer, k, 0)),
            pl.BlockSpec((tm, D_MODEL // nk), lambda i, k: (i, k)),
            _mod_spec(layer, gate_chunk, tm),
            _mod_spec(next_layer, next_chunk, tm),
            _mod_spec(next_layer, next_chunk + 1, tm),
        ],
        out_specs=(pl.BlockSpec((tm, D_MODEL), lambda i, k: (i, 0)),
                   pl.BlockSpec((tm, D_MODEL), lambda i, k: (i, 0))),
        compiler_params=_cparams(("parallel", "arbitrary"), 58),
        name="ffn_down",
    )(a, w_d, x, mod5, mod5, mod5)


def _cm_kernel(h_ref, w_ref, nw_ref, wsp_ref, bsp_ref, y_ref):
    p = _dot(h_ref[...], w_ref[...])
    u = jax.nn.gelu(p[:, :CM_WIDTH])
    v = jax.nn.gelu(p[:, CM_WIDTH:])
    vn = (_rms(v) * nw_ref[...]).astype(BF16)
    for c in range(h_ref.shape[0] // CHUNK):
        rows = slice(c * CHUNK, (c + 1) * CHUNK)
        for g in range(CM_GROUPS):
            cols = slice(g * CHUNK, (g + 1) * CHUNK)
            mixed = _dot(wsp_ref[g], vn[rows, cols]) + bsp_ref[g]
            y_ref[rows, cols] = (u[rows, cols] * mixed).astype(BF16)


def _cm_call(h, w_cm, cm_nw, wsp, bsp, layer):
    tm = 512
    return pl.pallas_call(
        _cm_kernel,
        out_shape=jax.ShapeDtypeStruct((T, CM_WIDTH), BF16),
        grid=(T // tm,),
        in_specs=[
            pl.BlockSpec((tm, D_MODEL), lambda i: (i, 0)),
            pl.BlockSpec((None, D_MODEL, 2 * CM_WIDTH), lambda i: (layer, 0, 0)),
            pl.BlockSpec((None, 1, CM_WIDTH), lambda i: (layer, 0, 0)),
            pl.BlockSpec((None, CM_GROUPS, CHUNK, CHUNK), lambda i: (layer, 0, 0, 0)),
            pl.BlockSpec((None, CM_GROUPS, CHUNK, CHUNK), lambda i: (layer, 0, 0, 0)),
        ],
        out_specs=pl.BlockSpec((tm, CM_WIDTH), lambda i: (i, 0)),
        compiler_params=_cparams(("parallel",), 48),
        name="chunk_mlp",
    )(h, w_cm, cm_nw, wsp, bsp)


def _retproj_kernel(h_ref, w_ref, s_ref, o_ref):
    o_ref[...] = (_dot(h_ref[...], w_ref[...]) * s_ref[...]).astype(BF16)


def _retproj_call(h, w_ret, col_scale, layer):
    tm, tn = 1024, 1024
    n = 2 * RET_QK_WIDTH + 2 * RET_V_WIDTH
    return pl.pallas_call(
        _retproj_kernel,
        out_shape=jax.ShapeDtypeStruct((T, n), BF16),
        grid=(T // tm, n // tn),
        in_specs=[
            pl.BlockSpec((tm, D_MODEL), lambda i, j: (i, 0)),
            pl.BlockSpec((None, D_MODEL, tn), lambda i, j: (layer, 0, j)),
            pl.BlockSpec((1, tn), lambda i, j: (0, j)),
        ],
        out_specs=pl.BlockSpec((tm, tn), lambda i, j: (i, j)),
        compiler_params=_cparams(("parallel", "parallel"), 40),
        name="ret_proj",
    )(h, w_ret, col_scale)


def _ret_kernel(*refs, layer, n_chunks, has_s0, emit_state):
    refs = list(refs)
    lg_ref, q_ref, k_ref, v_ref, g_ref, nw_ref = refs[:6]
    pos = 6
    s0_ref = None
    if has_s0:
        s0_ref = refs[pos]
        pos += 1
    pos += 1
    y_ref = refs[pos]
    pos += 1
    so_ref = None
    if emit_state:
        so_ref = refs[pos]
        pos += 1
    yacc_refs = refs[pos:pos + 2]

    head = pl.program_id(1)
    row = lax.broadcasted_iota(jnp.int32, (CHUNK, CHUNK), 0).astype(F32)
    col = lax.broadcasted_iota(jnp.int32, (CHUNK, CHUNK), 1).astype(F32)
    tables = []
    for d in range(2):
        lg = lg_ref[layer * 2 * RET_HEADS + d * RET_HEADS + head]
        if d == 0:
            rel = row - col
            q_decay = jnp.exp(lg * (row + 1.0))
            k_decay = jnp.exp(lg * (CHUNK - 1.0 - row))
        else:
            rel = col - row
            q_decay = jnp.exp(lg * (CHUNK - row))
            k_decay = jnp.exp(lg * row)
        intra = jnp.where(rel >= 0, jnp.exp(lg * jnp.maximum(rel, 0.0)), 0.0)
        chunk_decay = jnp.exp(jnp.full((RET_DK, RET_DV), lg * CHUNK, F32))
        tables.append((intra, q_decay, k_decay, chunk_decay))

    def body(t, states):
        new_states = []
        for d in range(2):
            intra, q_decay, k_decay, chunk_decay = tables[d]
            c = t if d == 0 else n_chunks - 1 - t
            rows = pl.ds(pl.multiple_of(c * CHUNK, CHUNK), CHUNK)
            qc = q_ref[rows, :]
            kc = k_ref[rows, :]
            vc = v_ref[rows, :]
            scores = (_dot_nt(qc, kc) * intra).astype(BF16)
            qd = (qc.astype(F32) * q_decay).astype(BF16)
            kd = (kc.astype(F32) * k_decay).astype(BF16)
            yacc_refs[d][rows, :] = _dot(scores, vc) + _dot(qd, states[d].astype(BF16))
            new_states.append(chunk_decay * states[d] + _dot_tn(kd, vc))
        return tuple(new_states)

    if has_s0:
        s_init = (s0_ref[0], s0_ref[1])
    else:
        s_init = (jnp.zeros((RET_DK, RET_DV), F32), jnp.zeros((RET_DK, RET_DV), F32))
    s_final = lax.fori_loop(0, n_chunks, body, s_init)
    if emit_state:
        so_ref[0] = s_final[0]
        so_ref[1] = s_final[1]

    y = yacc_refs[0][...] + yacc_refs[1][...]
    mu = jnp.mean(y, axis=-1, keepdims=True)
    var = jnp.mean(jnp.square(y - mu), axis=-1, keepdims=True)
    yn = (y - mu) * lax.rsqrt(var + EPS) * nw_ref[...]
    y_ref[...] = (jax.nn.silu(g_ref[...].astype(F32)) * yn).astype(BF16)


def _ret_call(lg, rp, ret_nw, s0, y_prev, layer, *, prompt):
    seq = SEQ if prompt else DEC_SEQ
    nb = BATCH if prompt else DEC_BATCH
    rb0 = 0 if prompt else T_P // DEC_SEQ
    kq = RET_QK_WIDTH // RET_DK
    kv = 2 * RET_QK_WIDTH // RET_DV
    kg = kv + RET_HEADS
    in_specs = [
        pl.BlockSpec(memory_space=pltpu.SMEM),
        pl.BlockSpec((seq, RET_DK), lambda b, h: (rb0 + b, h)),
        pl.BlockSpec((seq, RET_DK), lambda b, h: (rb0 + b, kq + h)),
        pl.BlockSpec((seq, RET_DV), lambda b, h: (rb0 + b, kv + h)),
        pl.BlockSpec((seq, RET_DV), lambda b, h: (rb0 + b, kg + h)),
        pl.BlockSpec((None, 1, RET_DV), lambda b, h: (layer, 0, h)),
    ]
    args = [lg, rp, rp, rp, rp, ret_nw]
    if not prompt:
        in_specs.append(pl.BlockSpec((None, None, 2, None, RET_DK, RET_DV), lambda b, h: (b, layer, 0, h, 0, 0)))
        args.append(s0)
    in_specs.append(pl.BlockSpec(memory_space=pl.ANY))
    args.append(y_prev)
    out_shape = [jax.ShapeDtypeStruct((T, RET_V_WIDTH), BF16)]
    out_specs = [pl.BlockSpec((seq, RET_DV), lambda b, h: (rb0 + b, h))]
    if prompt:
        out_shape.append(jax.ShapeDtypeStruct((BATCH, 2, RET_HEADS, RET_DK, RET_DV), F32))
        out_specs.append(pl.BlockSpec((None, 2, None, RET_DK, RET_DV), lambda b, h: (b, 0, h, 0, 0)))
    return pl.pallas_call(
        functools.partial(_ret_kernel, layer=layer, n_chunks=seq // CHUNK, has_s0=not prompt,
                          emit_state=prompt),
        out_shape=tuple(out_shape),
        grid=(nb, RET_HEADS),
        in_specs=in_specs,
        out_specs=tuple(out_specs),
        scratch_shapes=[pltpu.VMEM((seq, RET_DV), F32), pltpu.VMEM((seq, RET_DV), F32)],
        input_output_aliases={len(args) - 1: 0},
        compiler_params=_cparams(("parallel", "parallel"), 32),
        name="retention_prompt" if prompt else "retention_sample",
    )(*args)


def _rope_lanes(x, cos_t, sin_t):
    return x * cos_t + pltpu.roll(x, 64, 1) * sin_t


def _head_keys(kvf, kr2, rotk, wkn, k_ref, v_ref):
    lane = lax.broadcasted_iota(jnp.int32, (1, 2 * MLA_ROPE), 1)
    ssq_kr = jnp.sum(jnp.where(lane < MLA_ROPE, kr2 * kr2, 0.0), axis=-1, keepdims=True)
    nk = MLA_HEADS * MLA_NOPE
    for hh in range(MLA_HEADS):
        kn = kvf[:, hh * MLA_NOPE:(hh + 1) * MLA_NOPE]
        ssq = jnp.sum(kn * kn, axis=-1, keepdims=True) + ssq_kr
        r = lax.rsqrt(ssq / MLA_QK_DIM + EPS)
        k_ref[hh, :, 0:MLA_NOPE] = (kn * r * wkn).astype(BF16)
        k_ref[hh, :, MLA_NOPE:MLA_SLAB] = (rotk * r).astype(BF16)
        v_ref[hh] = kvf[:, nk + hh * MLA_V: nk + (hh + 1) * MLA_V].astype(BF16)


def _mla_kernel(h_ref, w1_ref, wuq_ref, wukv_ref, qnw_ref, kvnw_ref, wq_ref, wkn_ref, wkr_ref, cos_ref, sin_ref,
                q_ref, k_ref, v_ref, ckv_ref, kr_ref):
    p = _dot(h_ref[...], w1_ref[...])
    cq = _rms(p[:, :MLA_Q_RANK]) * qnw_ref[...]
    ckv = _rms(p[:, MLA_Q_RANK:MLA_Q_RANK + MLA_KV_RANK]) * kvnw_ref[...]
    kr2 = p[:, MLA_Q_RANK + MLA_KV_RANK:]
    ckv_ref[...] = ckv
    kr_ref[...] = kr2[:, :MLA_ROPE]
    cos_t = cos_ref[...]
    sin_t = sin_ref[...]
    qf = _dot(cq.astype(BF16), wuq_ref[...])
    kvf = _dot(ckv.astype(BF16), wukv_ref[...])
    lane = lax.broadcasted_iota(jnp.int32, (1, MLA_SLAB), 1)
    for hh in range(MLA_HEADS):
        slab = qf[:, hh * MLA_SLAB:(hh + 1) * MLA_SLAB]
        ssq = jnp.sum(jnp.where(lane < MLA_QK_DIM, slab * slab, 0.0), axis=-1, keepdims=True)
        n = slab * lax.rsqrt(ssq / MLA_QK_DIM + EPS) * wq_ref[...]
        q_ref[hh, :, 0:MLA_NOPE] = n[:, :MLA_NOPE].astype(BF16)
        q_ref[hh, :, MLA_NOPE:MLA_SLAB] = _rope_lanes(n[:, MLA_NOPE:], cos_t, sin_t).astype(BF16)
    rotk = _rope_lanes(kr2 * wkr_ref[...], cos_t, sin_t)
    _head_keys(kvf, kr2, rotk, wkn_ref[...], k_ref, v_ref)


def _mla_call(h, w1, wuq, wukv, qnw, kvnw, wq, wkn, wkr, cos_t, sin_t, layer):
    tm = 512
    n1 = MLA_Q_RANK + MLA_KV_RANK + 2 * MLA_ROPE
    lsel = lambda i: (layer, 0, 0)
    return pl.pallas_call(
        _mla_kernel,
        out_shape=(
            jax.ShapeDtypeStruct((MLA_HEADS, T, MLA_SLAB), BF16),
            jax.ShapeDtypeStruct((MLA_HEADS, T, MLA_SLAB), BF16),
            jax.ShapeDtypeStruct((MLA_HEADS, T, MLA_V), BF16),
            jax.ShapeDtypeStruct((T, MLA_KV_RANK), F32),
            jax.ShapeDtypeStruct((T, MLA_ROPE), F32),
        ),
        grid=(T // tm,),
        in_specs=[
            pl.BlockSpec((tm, D_MODEL), lambda i: (i, 0)),
            pl.BlockSpec((None, D_MODEL, n1), lsel),
            pl.BlockSpec((None, MLA_Q_RANK, MLA_HEADS * MLA_SLAB), lsel),
            pl.BlockSpec((None, MLA_KV_RANK, MLA_HEADS * (MLA_NOPE + MLA_V)), lsel),
            pl.BlockSpec((None, 1, MLA_Q_RANK), lsel),
            pl.BlockSpec((None, 1, MLA_KV_RANK), lsel),
            pl.BlockSpec((None, 1, MLA_SLAB), lsel),
            pl.BlockSpec((None, 1, MLA_NOPE), lsel),
            pl.BlockSpec((None, 1, 2 * MLA_ROPE), lsel),
            pl.BlockSpec((tm, 2 * MLA_ROPE), lambda i: (i, 0)),
            pl.BlockSpec((tm, 2 * MLA_ROPE), lambda i: (i, 0)),
        ],
        out_specs=(
            pl.BlockSpec((MLA_HEADS, tm, MLA_SLAB), lambda i: (0, i, 0)),
            pl.BlockSpec((MLA_HEADS, tm, MLA_SLAB), lambda i: (0, i, 0)),
            pl.BlockSpec((MLA_HEADS, tm, MLA_V), lambda i: (0, i, 0)),
            pl.BlockSpec((tm, MLA_KV_RANK), lambda i: (i, 0)),
            pl.BlockSpec((tm, MLA_ROPE), lambda i: (i, 0)),
        ),
        compiler_params=_cparams(("parallel",), 48),
        name="mla_proj",
    )(h, w1, wuq, wukv, qnw, kvnw, wq, wkn, wkr, cos_t, sin_t)


def _ctxkv_kernel(c_ref, kr_ref, wukv_ref, wkn_ref, wkr_ref, k_ref, v_ref):
    kvf = _dot(c_ref[...].astype(BF16), wukv_ref[...])
    kr2 = kr_ref[...]
    _head_keys(kvf, kr2, kr2 * wkr_ref[...], wkn_ref[...], k_ref, v_ref)


def _ctxkv_call(cache_ckv, cache_kr_pad, wukv, wkn, wkr):
    lsel = lambda l, b: (l, 0, 0)
    return pl.pallas_call(
        _ctxkv_kernel,
        out_shape=(
            jax.ShapeDtypeStruct((DEPTH, MLA_HEADS, DEC_BATCH * PAST_LEN, MLA_SLAB), BF16),
            jax.ShapeDtypeStruct((DEPTH, MLA_HEADS, DEC_BATCH * PAST_LEN, MLA_V), BF16),
        ),
        grid=(DEPTH, DEC_BATCH),
        in_specs=[
            pl.BlockSpec((None, None, PAST_LEN, MLA_KV_RANK), lambda l, b: (b, l, 0, 0)),
            pl.BlockSpec((None, None, PAST_LEN, 2 * MLA_ROPE), lambda l, b: (b, l, 0, 0)),
            pl.BlockSpec((None, MLA_KV_RANK, MLA_HEADS * (MLA_NOPE + MLA_V)), lsel),
            pl.BlockSpec((None, 1, MLA_NOPE), lsel),
            pl.BlockSpec((None, 1, 2 * MLA_ROPE), lsel),
        ],
        out_specs=(
            pl.BlockSpec((None, MLA_HEADS, PAST_LEN, MLA_SLAB), lambda l, b: (l, 0, b, 0)),
            pl.BlockSpec((None, MLA_HEADS, PAST_LEN, MLA_V), lambda l, b: (l, 0, b, 0)),
        ),
        compiler_params=_cparams(("parallel", "parallel"), 32),
        name="ctx_kv",
    )(cache_ckv, cache_kr_pad, wukv, wkn, wkr)


def _attend(q, keys, values):
    scores = [_dot_nt(q, k) for k in keys]
    m = functools.reduce(jnp.maximum, [jnp.max(s, axis=-1, keepdims=True) for s in scores])
    probs = [jnp.exp(s - m) for s in scores]
    denom = sum(jnp.sum(p, axis=-1, keepdims=True) for p in probs)
    o = sum(_dot(p.astype(BF16), v) for p, v in zip(probs, values))
    return (o / denom).astype(BF16)


ATTN_ROWS = 256


def _attn_prompt_kernel(q_ref, k_ref, v_ref, _, o_ref):
    for hh in range(MLA_HEADS):
        o_ref[:, hh * MLA_V:(hh + 1) * MLA_V] = _attend(q_ref[hh], [k_ref[hh]], [v_ref[hh]])


def _attn_sample_kernel(q_ref, ks_ref, vs_ref, kc_ref, vc_ref, _, o_ref):
    for r in range(q_ref.shape[0] // ATTN_ROWS):
        rows = slice(r * ATTN_ROWS, (r + 1) * ATTN_ROWS)
        o_ref[rows, :] = _attend(q_ref[rows, :], [ks_ref[...], kc_ref[...]], [vs_ref[...], vc_ref[...]])


def _attn_prompt_call(q, k, v, y_prev):
    return pl.pallas_call(
        _attn_prompt_kernel,
        out_shape=jax.ShapeDtypeStruct((T, MLA_HEADS * MLA_V), BF16),
        grid=(BATCH,),
        in_specs=[
            pl.BlockSpec((MLA_HEADS, SEQ, MLA_SLAB), lambda b: (0, b, 0)),
            pl.BlockSpec((MLA_HEADS, SEQ, MLA_SLAB), lambda b: (0, b, 0)),
            pl.BlockSpec((MLA_HEADS, SEQ, MLA_V), lambda b: (0, b, 0)),
            pl.BlockSpec(memory_space=pl.ANY),
        ],
        out_specs=pl.BlockSpec((SEQ, MLA_HEADS * MLA_V), lambda b: (b, 0)),
        input_output_aliases={3: 0},
        compiler_params=_cparams(("parallel",), 32),
        name="attn_prompt",
    )(q, k, v, y_prev)


def _attn_sample_call(q, k, v, kc, vc, y_prev, layer):
    tq = 1024
    nq = DEC_SEQ // tq
    q0 = T_P // tq
    s0 = T_P // DEC_SEQ
    return pl.pallas_call(
        _attn_sample_kernel,
        out_shape=jax.ShapeDtypeStruct((T, MLA_HEADS * MLA_V), BF16),
        grid=(DEC_BATCH, MLA_HEADS, nq),
        in_specs=[
            pl.BlockSpec((None, tq, MLA_SLAB), lambda b, h, i: (h, q0 + b * nq + i, 0)),
            pl.BlockSpec((None, DEC_SEQ, MLA_SLAB), lambda b, h, i: (h, s0 + b, 0)),
            pl.BlockSpec((None, DEC_SEQ, MLA_V), lambda b, h, i: (h, s0 + b, 0)),
            pl.BlockSpec((None, None, PAST_LEN, MLA_SLAB), lambda b, h, i: (layer, h, b, 0)),
            pl.BlockSpec((None, None, PAST_LEN, MLA_V), lambda b, h, i: (layer, h, b, 0)),
            pl.BlockSpec(memory_space=pl.ANY),
        ],
        out_specs=pl.BlockSpec((tq, MLA_V), lambda b, h, i: (q0 + b * nq + i, h)),
        input_output_aliases={5: 0},
        compiler_params=_cparams(("parallel", "parallel", "parallel"), 48),
        name="attn_sample",
    )(q, k, v, kc, vc, y_prev)


def _merge_kernel(h_ref, ya_ref, yb_ref, yc_ref, wga_ref, wgb_ref, wgc_ref, wb_ref, o_ref):
    h = h_ref[...]
    m = jax.nn.sigmoid(_dot(h, wga_ref[...])) * _dot(ya_ref[...], wb_ref[0])
    m = m + jax.nn.sigmoid(_dot(h, wgb_ref[...])) * _dot(yb_ref[...], wb_ref[1])
    m = m + jax.nn.sigmoid(_dot(h, wgc_ref[...])) * _dot(yc_ref[...], wb_ref[2])
    o_ref[...] = m.astype(BF16)


def _merge_call(h, ya, yb, yc, w_g, w_b, layer):
    tm, tn = 1024, 512
    nj = D_MODEL // tn
    yspec = pl.BlockSpec((tm, BRANCH_WIDTH), lambda i, j: (i, 0))
    return pl.pallas_call(
        _merge_kernel,
        out_shape=jax.ShapeDtypeStruct((T, D_MODEL), BF16),
        grid=(T // tm, nj),
        in_specs=[
            pl.BlockSpec((tm, D_MODEL), lambda i, j: (i, 0)),
            yspec, yspec, yspec,
            pl.BlockSpec((None, D_MODEL, tn), lambda i, j: (layer, 0, j)),
            pl.BlockSpec((None, D_MODEL, tn), lambda i, j: (layer, 0, j + nj)),
            pl.BlockSpec((None, D_MODEL, tn), lambda i, j: (layer, 0, j + 2 * nj)),
            pl.BlockSpec((None, 3, BRANCH_WIDTH, tn), lambda i, j: (layer, 0, 0, j)),
        ],
        out_specs=pl.BlockSpec((tm, tn), lambda i, j: (i, j)),
        compiler_params=_cparams(("parallel", "parallel"), 56),
        name="gated_merge",
    )(h, ya, yb, yc, w_g, w_g, w_g, w_b)


def _outproj_kernel(m_ref, w_ref, x_ref, g_ref, sh_ref, sc_ref, xo_ref, ho_ref):
    xn = x_ref[...] + g_ref[...] * _dot(m_ref[...], w_ref[...])
    xo_ref[...] = xn
    ho_ref[...] = _norm_mod(xn, sh_ref[...], sc_ref[...]).astype(BF16)


def _outproj_call(m, w_out, x, mod5, layer):
    tm = 512
    return pl.pallas_call(
        _outproj_kernel,
        out_shape=(jax.ShapeDtypeStruct((T, D_MODEL), F32), jax.ShapeDtypeStruct((T, D_MODEL), BF16)),
        grid=(T // tm,),
        in_specs=[
            pl.BlockSpec((tm, D_MODEL), lambda i: (i, 0)),
            pl.BlockSpec((None, D_MODEL, D_MODEL), lambda i: (layer, 0, 0)),
            pl.BlockSpec((tm, D_MODEL), lambda i: (i, 0)),
            _mod_spec(layer, 5, tm),
            _mod_spec(layer, 6, tm),
            _mod_spec(layer, 7, tm),
        ],
        out_specs=(pl.BlockSpec((tm, D_MODEL), lambda i: (i, 0)),
                   pl.BlockSpec((tm, D_MODEL), lambda i: (i, 0))),
        compiler_params=_cparams(("parallel",), 52),
        name="out_proj",
    )(m, w_out, x, mod5, mod5, mod5)


def _swap_halves(x):
    half = x.shape[-1] // 2
    return jnp.concatenate([x[..., half:], x[..., :half]], axis=-1)


def _rope_tables():
    t = jnp.arange(DEC_SEQ)
    row = (t // GRID_W).astype(F32)
    col = (t % GRID_W).astype(F32)
    n_freq = MLA_ROPE // 4
    freqs = jnp.power(ROPE_THETA, -jnp.arange(n_freq, dtype=F32) / n_freq)
    ang = jnp.concatenate([row[:, None] * freqs, col[:, None] * freqs], axis=-1)
    cos, sin = jnp.cos(ang), jnp.sin(ang)
    zeros = jnp.zeros((DEC_SEQ, MLA_ROPE), F32)
    cos_s = jnp.tile(jnp.concatenate([cos, cos, zeros], axis=-1), (DEC_BATCH, 1))
    sin_s = jnp.tile(jnp.concatenate([-sin, sin, zeros], axis=-1), (DEC_BATCH, 1))
    cos_p = jnp.concatenate([jnp.ones((T_P, MLA_ROPE), F32), jnp.zeros((T_P, MLA_ROPE), F32)], axis=-1)
    sin_p = jnp.zeros((T_P, 2 * MLA_ROPE), F32)
    return jnp.concatenate([cos_p, cos_s], axis=0), jnp.concatenate([sin_p, sin_s], axis=0)


def kernel(x_prompt, x_sample, cache_mla_ckv, cache_mla_krope, state_retention, c, c_ctx, w_mod, b_mod,
           ffn1_w_gate_up, ffn1_w_down, ffn2_w_gate_up, ffn2_w_down, w_in, cm_norm_w, cm_w_spatial,
           cm_b_spatial, ret_decay_logit, ret_norm_w, mla_q_norm_w, mla_kv_norm_w, mla_w_uq, mla_w_ukv,
           qk_norm_q_w, qk_norm_k_w, w_branch, w_out):
    o_cm, o_ret, o_mla, o_kr, o_gate = 0, 2 * CM_WIDTH, 2 * CM_WIDTH + 2 * RET_QK_WIDTH + 2 * RET_V_WIDTH, 0, 0
    o_kr = o_mla + MLA_Q_RANK + MLA_KV_RANK
    o_gate = o_kr + MLA_ROPE
    w_cm = w_in[:, :, o_cm:o_ret].astype(BF16)
    w_ret = w_in[:, :, o_ret:o_mla].astype(BF16)
    w_kr = w_in[:, :, o_kr:o_gate]
    w_mla1 = jnp.concatenate([w_in[:, :, o_mla:o_kr], w_kr, _swap_halves(w_kr)], axis=-1).astype(BF16)
    w_gate = w_in[:, :, o_gate:].astype(BF16)
    ffn1_gu = ffn1_w_gate_up.astype(BF16)
    ffn1_d = ffn1_w_down.astype(BF16)
    ffn2_gu = ffn2_w_gate_up.astype(BF16)
    ffn2_d = ffn2_w_down.astype(BF16)
    w_b = w_branch.astype(BF16)
    w_o = w_out.astype(BF16)

    uq = mla_w_uq.reshape(DEPTH, MLA_Q_RANK, MLA_HEADS, MLA_QK_DIM)
    uq_rope = uq[..., MLA_NOPE:]
    wuq = jnp.concatenate([uq[..., :MLA_NOPE], uq_rope, _swap_halves(uq_rope)], axis=-1)
    wuq = wuq.reshape(DEPTH, MLA_Q_RANK, MLA_HEADS * MLA_SLAB).astype(BF16)
    ukv = mla_w_ukv.reshape(DEPTH, MLA_KV_RANK, MLA_HEADS, MLA_NOPE + MLA_V)
    wukv = jnp.concatenate([ukv[..., :MLA_NOPE].reshape(DEPTH, MLA_KV_RANK, MLA_HEADS * MLA_NOPE),
                            ukv[..., MLA_NOPE:].reshape(DEPTH, MLA_KV_RANK, MLA_HEADS * MLA_V)],
                           axis=-1).astype(BF16)
    q_rope_w = qk_norm_q_w[:, MLA_NOPE:]
    wq = (jnp.concatenate([qk_norm_q_w[:, :MLA_NOPE], q_rope_w, _swap_halves(q_rope_w)], axis=-1)
          * (MLA_QK_DIM ** -0.5)).reshape(DEPTH, 1, MLA_SLAB)
    k_rope_w = qk_norm_k_w[:, MLA_NOPE:]
    wkn = qk_norm_k_w[:, :MLA_NOPE].reshape(DEPTH, 1, MLA_NOPE)
    wkr = jnp.concatenate([k_rope_w, _swap_halves(k_rope_w)], axis=-1).reshape(DEPTH, 1, 2 * MLA_ROPE)
    qnw = mla_q_norm_w.reshape(DEPTH, 1, MLA_Q_RANK)
    kvnw = mla_kv_norm_w.reshape(DEPTH, 1, MLA_KV_RANK)
    cm_nw = cm_norm_w.reshape(DEPTH, 1, CM_WIDTH)
    wsp = cm_w_spatial.astype(BF16)
    bsp = jnp.broadcast_to(cm_b_spatial[..., None], (DEPTH, CM_GROUPS, CHUNK, CHUNK))
    ret_nw = ret_norm_w.reshape(DEPTH, 1, RET_V_WIDTH)
    lg = jax.nn.log_sigmoid(ret_decay_logit.astype(F32)).reshape(-1)
    ret_scale = jnp.concatenate([jnp.ones((RET_QK_WIDTH,), F32), jnp.full((RET_QK_WIDTH,), RET_DK ** -0.5, F32),
                                 jnp.ones((2 * RET_V_WIDTH,), F32)]).reshape(1, -1)
    cos_t, sin_t = _rope_tables()
    cache_kr_pad = jnp.pad(cache_mla_krope, ((0, 0), (0, 0), (0, 0), (0, MLA_ROPE)))

    cond = jnp.concatenate([c_ctx[None, :], c, jnp.zeros((MOD_ROWS - 1 - DEC_BATCH, D_MODEL), F32)], axis=0)
    x = jnp.concatenate([x_prompt.reshape(T_P, D_MODEL), x_sample.reshape(T_S, D_MODEL)], axis=0)

    mod5 = _mod_call(cond, w_mod, b_mod).reshape(DEPTH, MOD_ROWS, N_MOD, 1, D_MODEL)
    kc, vc = _ctxkv_call(cache_mla_ckv, cache_kr_pad, wukv, wkn, wkr)
    h = _prenorm_call(x, mod5)

    ckv_layers, krope_layers, ret_layers = [], [], []
    for l in range(DEPTH):
        x, h = _ffn_call(h, x, ffn1_gu, ffn1_d, mod5, l, 2, l, 3)

        ya = _cm_call(h, w_cm, cm_nw, wsp, bsp, l)

        rp = _retproj_call(h, w_ret, ret_scale, l)
        yb_init = jnp.zeros((T, RET_V_WIDTH), BF16)
        yb, ret_state = _ret_call(lg, rp, ret_nw, None, yb_init, l, prompt=True)
        (yb,) = _ret_call(lg, rp, ret_nw, state_retention, yb, l, prompt=False)

        q, k, v, ckv, kr = _mla_call(h, w_mla1, wuq, wukv, qnw, kvnw, wq, wkn, wkr, cos_t, sin_t, l)
        yc_init = jnp.zeros((T, MLA_HEADS * MLA_V), BF16)
        yc = _attn_prompt_call(q, k, v, yc_init)
        yc = _attn_sample_call(q, k, v, kc, vc, yc, l)

        merged = _merge_call(h, ya, yb, yc, w_gate, w_b, l)
        x, h = _outproj_call(merged, w_o, x, mod5, l)

        nl = min(l + 1, DEPTH - 1)
        x, h = _ffn_call(h, x, ffn2_gu, ffn2_d, mod5, l, 8, nl, 0)

        ckv_layers.append(ckv[:T_P].reshape(BATCH, SEQ, MLA_KV_RANK))
        krope_layers.append(kr[:T_P].reshape(BATCH, SEQ, MLA_ROPE))
        ret_layers.append(ret_state)

    y_prompt = x[:T_P].reshape(BATCH, SEQ, D_MODEL)
    y_sample = x[T_P:].reshape(DEC_BATCH, DEC_SEQ, D_MODEL)
    return (y_prompt, y_sample, jnp.stack(ckv_layers, axis=1), jnp.stack(krope_layers, axis=1),
            jnp.stack(ret_layers, axis=1))
```

```python
import functools

import jax
import jax.numpy as jnp
from jax import lax
from jax.experimental import pallas as pl
from jax.experimental.pallas import tpu as pltpu

F32 = jnp.float32
BF16 = jnp.bfloat16

D_MODEL = 2048
BATCH = 16
SEQ = 256
DEPTH = 4
DEC_BATCH = 8
DEC_SEQ = 2048
PAST_LEN = 512
GRID_W = 64
CHUNK = 128
D_FF = 5632
N_MOD = 9
EPS = 1e-6
ROPE_THETA = 10000.0
CM_GROUPS = 8
CM_WIDTH = 1024
RET_HEADS = 4
RET_DK = 128
RET_DV = 256
RET_QK_WIDTH = RET_HEADS * RET_DK
RET_V_WIDTH = RET_HEADS * RET_DV
MLA_HEADS = 8
MLA_NOPE = 128
MLA_ROPE = 64
MLA_V = 128
MLA_Q_RANK = 512
MLA_KV_RANK = 512
MLA_QK_DIM = MLA_NOPE + MLA_ROPE
MLA_SLAB = 256
BRANCH_WIDTH = 1024

W_RET_COL = 2 * CM_WIDTH
W_MLA_COL = W_RET_COL + 2 * RET_QK_WIDTH + 2 * RET_V_WIDTH
W_KR_COL = W_MLA_COL + MLA_Q_RANK + MLA_KV_RANK
W_GATE_COL = W_KR_COL + MLA_ROPE

T_P = BATCH * SEQ
T_S = DEC_BATCH * DEC_SEQ
T = T_P + T_S
MOD_ROWS = 16

MIB = 1024 * 1024


def _cparams(semantics, vmem_mib):
    return pltpu.CompilerParams(dimension_semantics=semantics, vmem_limit_bytes=vmem_mib * MIB)


def _mod_row(i, tm):
    n_p = T_P // tm
    return jnp.where(i < n_p, 0, 1 + (i - n_p) // (DEC_SEQ // tm))


def _mod_spec(layer, chunk, tm):
    return pl.BlockSpec((None, None, None, 1, D_MODEL),
                        lambda i, *_: (layer, _mod_row(i, tm), chunk, 0, 0))


def _rms(x):
    return x * lax.rsqrt(jnp.mean(x * x, axis=-1, keepdims=True) + EPS)


def _norm_mod(x, sh, sc):
    return _rms(x) * (1 + sc) + sh


def _dot(a, b):
    return jnp.dot(a, b, preferred_element_type=F32)


def _dot_nt(a, b):
    return lax.dot_general(a, b, (((1,), (1,)), ((), ())), preferred_element_type=F32)


def _dot_tn(a, b):
    return lax.dot_general(a, b, (((0,), (0,)), ((), ())), preferred_element_type=F32)


def _mod_kernel(c_ref, w_ref, b_ref, o_ref):
    s = jax.nn.silu(c_ref[...]).astype(BF16)
    o_ref[...] = _dot(s, w_ref[...].astype(BF16)) + b_ref[...]


def _mod_call(cond, w_mod, b_mod):
    tn = 1024
    n = N_MOD * D_MODEL
    return pl.pallas_call(
        _mod_kernel,
        out_shape=jax.ShapeDtypeStruct((DEPTH, MOD_ROWS, n), F32),
        grid=(DEPTH, n // tn),
        in_specs=[
            pl.BlockSpec((MOD_ROWS, D_MODEL), lambda l, j: (0, 0)),
            pl.BlockSpec((None, D_MODEL, tn), lambda l, j: (l, 0, j)),
            pl.BlockSpec((None, 1, tn), lambda l, j: (l, 0, j)),
        ],
        out_specs=pl.BlockSpec((None, MOD_ROWS, tn), lambda l, j: (l, 0, j)),
        compiler_params=_cparams(("parallel", "parallel"), 40),
        name="mod",
    )(cond, w_mod, b_mod.reshape(DEPTH, 1, n))


def _prenorm_kernel(x_ref, sh_ref, sc_ref, h_ref):
    h_ref[...] = _norm_mod(x_ref[...], sh_ref[...], sc_ref[...]).astype(BF16)


def _prenorm_call(x, mod5):
    tm = 512
    return pl.pallas_call(
        _prenorm_kernel,
        out_shape=jax.ShapeDtypeStruct((T, D_MODEL), BF16),
        grid=(T // tm,),
        in_specs=[
            pl.BlockSpec((tm, D_MODEL), lambda i: (i, 0)),
            _mod_spec(0, 0, tm),
            _mod_spec(0, 1, tm),
        ],
        out_specs=pl.BlockSpec((tm, D_MODEL), lambda i: (i, 0)),
        compiler_params=_cparams(("parallel",), 32),
        name="prenorm",
    )(x, mod5, mod5)


def _ffn_up_kernel(h_ref, wg_ref, wu_ref, a_ref):
    h = h_ref[...]
    gate = _dot(h, wg_ref[...].astype(BF16))
    up = _dot(h, wu_ref[...].astype(BF16))
    a_ref[...] = (jax.nn.silu(gate) * up).astype(BF16)


def _ffn_down_kernel(a_ref, wd_ref, xs_ref, g_ref, sh_ref, sc_ref, xo_ref, ho_ref, *, nk):
    k = pl.program_id(1)
    tn = D_MODEL // nk
    half_g = 0.5 * g_ref[...]
    chunks = [slice(n * tn, (n + 1) * tn) for n in range(nk)]

    def step_product(cols):
        return half_g[:, cols] * _dot(a_ref[...], wd_ref[:, cols])

    @pl.when(k == 0)
    def _():
        xo_ref[...] = jnp.zeros_like(xo_ref)

    @pl.when(k < nk - 1)
    def _():
        for cols in chunks:
            xo_ref[:, cols] += step_product(cols)
        for n in range(nk - 1):
            @pl.when(k == n)
            def _(n=n):
                xo_ref[:, chunks[n]] += xs_ref[...]

    @pl.when(k == nk - 1)
    def _():
        ssq = jnp.zeros((xo_ref.shape[0], 1), F32)
        for n, cols in enumerate(chunks):
            val = xo_ref[:, cols] + step_product(cols)
            if n == nk - 1:
                val = val + xs_ref[...]
            xo_ref[:, cols] = val
            ssq = ssq + jnp.sum(val * val, axis=-1, keepdims=True)
        r = lax.rsqrt(ssq / D_MODEL + EPS)
        for cols in chunks:
            y = xo_ref[:, cols] * r * (1 + sc_ref[:, cols]) + sh_ref[:, cols]
            ho_ref[:, cols] = y.astype(BF16)


def _ffn_call(h, x, w_gu, w_d, mod5, layer, gate_chunk, next_layer, next_chunk):
    tm, tf = 2048, 512
    nf = D_FF // tf
    a = pl.pallas_call(
        _ffn_up_kernel,
        out_shape=jax.ShapeDtypeStruct((T, D_FF), BF16),
        grid=(T // tm, nf),
        in_specs=[
            pl.BlockSpec((tm, D_MODEL), lambda i, j: (i, 0)),
            pl.BlockSpec((None, D_MODEL, tf), lambda i, j: (layer, 0, j)),
            pl.BlockSpec((None, D_MODEL, tf), lambda i, j: (layer, 0, j + nf)),
        ],
        out_specs=pl.BlockSpec((tm, tf), lambda i, j: (i, j)),
        compiler_params=_cparams(("parallel", "parallel"), 52),
        name="ffn_up",
    )(h, w_gu, w_gu)
    tm, nk = 1024, 4
    tk = D_FF // nk
    return pl.pallas_call(
        functools.partial(_ffn_down_kernel, nk=nk),
        out_shape=(jax.ShapeDtypeStruct((T, D_MODEL), F32), jax.ShapeDtypeStruct((T, D_MODEL), BF16)),
        grid=(T // tm, nk),
        in_specs=[
            pl.BlockSpec((tm, tk), lambda i, k: (i, k)),
            pl.BlockSpec((None, tk, D_MODEL), lambda i, k: (layer, k, 0)),
            pl.BlockSpec((tm, D_MODEL // nk), lambda i, k: (i, k)),
            _mod_spec(layer, gate_chunk, tm),
            _mod_spec(next_layer, next_chunk, tm),
            _mod_spec(next_layer, next_chunk + 1, tm),
        ],
        out_specs=(pl.BlockSpec((tm, D_MODEL), lambda i, k: (i, 0)),
                   pl.BlockSpec((tm, D_MODEL), lambda i, k: (i, 0))),
        compiler_params=_cparams(("parallel", "arbitrary"), 58),
        name="ffn_down",
    )(a, w_d, x, mod5, mod5, mod5)


def _cm_kernel(h_ref, w_ref, nw_ref, wsp_ref, bsp_ref, y_ref):
    p = _dot(h_ref[...], w_ref[...])
    u = jax.nn.gelu(p[:, :CM_WIDTH])
    v = jax.nn.gelu(p[:, CM_WIDTH:])
    vn = (_rms(v) * nw_ref[...]).astype(BF16)
    for c in range(h_ref.shape[0] // CHUNK):
        rows = slice(c * CHUNK, (c + 1) * CHUNK)
        for g in range(CM_GROUPS):
            cols = slice(g * CHUNK, (g + 1) * CHUNK)
            mixed = _dot(wsp_ref[g], vn[rows, cols]) + bsp_ref[g]
            y_ref[rows, cols] = (u[rows, cols] * mixed).astype(BF16)


def _cm_call(h, w_cm, cm_nw, wsp, bsp, layer):
    tm = 512
    return pl.pallas_call(
        _cm_kernel,
        out_shape=jax.ShapeDtypeStruct((T, CM_WIDTH), BF16),
        grid=(T // tm,),
        in_specs=[
            pl.BlockSpec((tm, D_MODEL), lambda i: (i, 0)),
            pl.BlockSpec((None, D_MODEL, 2 * CM_WIDTH), lambda i: (layer, 0, 0)),
            pl.BlockSpec((None, 1, CM_WIDTH), lambda i: (layer, 0, 0)),
            pl.BlockSpec((None, CM_GROUPS, CHUNK, CHUNK), lambda i: (layer, 0, 0, 0)),
            pl.BlockSpec((None, CM_GROUPS, CHUNK, CHUNK), lambda i: (layer, 0, 0, 0)),
        ],
        out_specs=pl.BlockSpec((tm, CM_WIDTH), lambda i: (i, 0)),
        compiler_params=_cparams(("parallel",), 48),
        name="chunk_mlp",
    )(h, w_cm, cm_nw, wsp, bsp)


def _retproj_kernel(h_ref, w_ref, s_ref, o_ref):
    o_ref[...] = (_dot(h_ref[...], w_ref[...]) * s_ref[...]).astype(BF16)


def _retproj_call(h, w_ret, col_scale, layer):
    tm, tn = 1024, 1024
    n = 2 * RET_QK_WIDTH + 2 * RET_V_WIDTH
    return pl.pallas_call(
        _retproj_kernel,
        out_shape=jax.ShapeDtypeStruct((T, n), BF16),
        grid=(T // tm, n // tn),
        in_specs=[
            pl.BlockSpec((tm, D_MODEL), lambda i, j: (i, 0)),
            pl.BlockSpec((None, D_MODEL, tn), lambda i, j: (layer, 0, W_RET_COL // tn + j)),
            pl.BlockSpec((1, tn), lambda i, j: (0, j)),
        ],
        out_specs=pl.BlockSpec((tm, tn), lambda i, j: (i, j)),
        compiler_params=_cparams(("parallel", "parallel"), 40),
        name="ret_proj",
    )(h, w_ret, col_scale)


def _ret_kernel(*refs, layer, n_chunks, has_s0, emit_state):
    refs = list(refs)
    lg_ref, q_ref, k_ref, v_ref, g_ref, nw_ref = refs[:6]
    pos = 6
    s0_ref = None
    if has_s0:
        s0_ref = refs[pos]
        pos += 1
    pos += 1
    y_ref = refs[pos]
    pos += 1
    so_ref = None
    if emit_state:
        so_ref = refs[pos]
        pos += 1
    yacc_refs = refs[pos:pos + 2]

    head = pl.program_id(1)
    row = lax.broadcasted_iota(jnp.int32, (CHUNK, CHUNK), 0).astype(F32)
    col = lax.broadcasted_iota(jnp.int32, (CHUNK, CHUNK), 1).astype(F32)
    tables = []
    for d in range(2):
        lg = lg_ref[layer * 2 * RET_HEADS + d * RET_HEADS + head]
        if d == 0:
            rel = row - col
            q_decay = jnp.exp(lg * (row + 1.0))
            k_decay = jnp.exp(lg * (CHUNK - 1.0 - row))
        else:
            rel = col - row
            q_decay = jnp.exp(lg * (CHUNK - row))
            k_decay = jnp.exp(lg * row)
        intra = jnp.where(rel >= 0, jnp.exp(lg * jnp.maximum(rel, 0.0)), 0.0)
        chunk_decay = jnp.exp(jnp.full((RET_DK, RET_DV), lg * CHUNK, F32))
        tables.append((intra, q_decay, k_decay, chunk_decay))

    def body(t, states):
        new_states = []
        for d in range(2):
            intra, q_decay, k_decay, chunk_decay = tables[d]
            c = t if d == 0 else n_chunks - 1 - t
            rows = pl.ds(pl.multiple_of(c * CHUNK, CHUNK), CHUNK)
            qc = q_ref[rows, :]
            kc = k_ref[rows, :]
            vc = v_ref[rows, :]
            scores = (_dot_nt(qc, kc) * intra).astype(BF16)
            qd = (qc.astype(F32) * q_decay).astype(BF16)
            kd = (kc.astype(F32) * k_decay).astype(BF16)
            yacc_refs[d][rows, :] = _dot(scores, vc) + _dot(qd, states[d].astype(BF16))
            new_states.append(chunk_decay * states[d] + _dot_tn(kd, vc))
        return tuple(new_states)

    if has_s0:
        s_init = (s0_ref[0], s0_ref[1])
    else:
        s_init = (jnp.zeros((RET_DK, RET_DV), F32), jnp.zeros((RET_DK, RET_DV), F32))
    s_final = lax.fori_loop(0, n_chunks, body, s_init, unroll=min(4, n_chunks))
    if emit_state:
        so_ref[0] = s_final[0]
        so_ref[1] = s_final[1]

    y = yacc_refs[0][...] + yacc_refs[1][...]
    mu = jnp.mean(y, axis=-1, keepdims=True)
    var = jnp.mean(jnp.square(y - mu), axis=-1, keepdims=True)
    yn = (y - mu) * lax.rsqrt(var + EPS) * nw_ref[...]
    y_ref[...] = (jax.nn.silu(g_ref[...].astype(F32)) * yn).astype(BF16)


def _ret_call(lg, rp, ret_nw, s0, y_prev, layer, *, prompt):
    seq = SEQ if prompt else DEC_SEQ
    nb = BATCH if prompt else DEC_BATCH
    rb0 = 0 if prompt else T_P // DEC_SEQ
    kq = RET_QK_WIDTH // RET_DK
    kv = 2 * RET_QK_WIDTH // RET_DV
    kg = kv + RET_HEADS
    in_specs = [
        pl.BlockSpec(memory_space=pltpu.SMEM),
        pl.BlockSpec((seq, RET_DK), lambda b, h: (rb0 + b, h)),
        pl.BlockSpec((seq, RET_DK), lambda b, h: (rb0 + b, kq + h)),
        pl.BlockSpec((seq, RET_DV), lambda b, h: (rb0 + b, kv + h)),
        pl.BlockSpec((seq, RET_DV), lambda b, h: (rb0 + b, kg + h)),
        pl.BlockSpec((None, 1, RET_DV), lambda b, h: (layer, 0, h)),
    ]
    args = [lg, rp, rp, rp, rp, ret_nw]
    if not prompt:
        in_specs.append(pl.BlockSpec((None, None, 2, None, RET_DK, RET_DV), lambda b, h: (b, layer, 0, h, 0, 0)))
        args.append(s0)
    in_specs.append(pl.BlockSpec(memory_space=pl.ANY))
    args.append(y_prev)
    out_shape = [jax.ShapeDtypeStruct((T, RET_V_WIDTH), BF16)]
    out_specs = [pl.BlockSpec((seq, RET_DV), lambda b, h: (rb0 + b, h))]
    if prompt:
        out_shape.append(jax.ShapeDtypeStruct((BATCH, 2, RET_HEADS, RET_DK, RET_DV), F32))
        out_specs.append(pl.BlockSpec((None, 2, None, RET_DK, RET_DV), lambda b, h: (b, 0, h, 0, 0)))
    return pl.pallas_call(
        functools.partial(_ret_kernel, layer=layer, n_chunks=seq // CHUNK, has_s0=not prompt,
                          emit_state=prompt),
        out_shape=tuple(out_shape),
        grid=(nb, RET_HEADS),
        in_specs=in_specs,
        out_specs=tuple(out_specs),
        scratch_shapes=[pltpu.VMEM((seq, RET_DV), F32), pltpu.VMEM((seq, RET_DV), F32)],
        input_output_aliases={len(args) - 1: 0},
        compiler_params=_cparams(("parallel", "parallel"), 32),
        name="retention_prompt" if prompt else "retention_sample",
    )(*args)


def _rope_lanes(x, cos_t, sin_t):
    return x * cos_t + pltpu.roll(x, 64, 1) * sin_t


def _head_keys(kvf, kr2, rotk, wkn, k_ref, v_ref, rows):
    lane = lax.broadcasted_iota(jnp.int32, (1, 2 * MLA_ROPE), 1)
    ssq_kr = jnp.sum(jnp.where(lane < MLA_ROPE, kr2 * kr2, 0.0), axis=-1, keepdims=True)
    nk = MLA_HEADS * MLA_NOPE
    for hh in range(MLA_HEADS):
        kn = kvf[:, hh * MLA_NOPE:(hh + 1) * MLA_NOPE]
        ssq = jnp.sum(kn * kn, axis=-1, keepdims=True) + ssq_kr
        r = lax.rsqrt(ssq / MLA_QK_DIM + EPS)
        k_ref[hh, rows, 0:MLA_NOPE] = (kn * r * wkn).astype(BF16)
        k_ref[hh, rows, MLA_NOPE:MLA_SLAB] = (rotk * r).astype(BF16)
        v_ref[hh, rows, :] = kvf[:, nk + hh * MLA_V: nk + (hh + 1) * MLA_V].astype(BF16)


MLA_ROWS = 256


def _mla_kernel(h_ref, w1_ref, wuq_ref, wukv_ref, qnw_ref, kvnw_ref, wq_ref, wkn_ref, wkr_ref, cos_ref, sin_ref,
                q_ref, k_ref, v_ref, ckv_ref, kr_ref):
    lane = lax.broadcasted_iota(jnp.int32, (1, MLA_SLAB), 1)
    for c in range(h_ref.shape[0] // MLA_ROWS):
        rows = slice(c * MLA_ROWS, (c + 1) * MLA_ROWS)
        p = _dot(h_ref[rows, :], w1_ref[...])
        cq = _rms(p[:, :MLA_Q_RANK]) * qnw_ref[...]
        ckv = _rms(p[:, MLA_Q_RANK:MLA_Q_RANK + MLA_KV_RANK]) * kvnw_ref[...]
        kr2 = p[:, MLA_Q_RANK + MLA_KV_RANK:]
        ckv_ref[rows, :] = ckv
        kr_ref[rows, :] = kr2[:, :MLA_ROPE]
        cos_t = cos_ref[rows, :]
        sin_t = sin_ref[rows, :]
        qf = _dot(cq.astype(BF16), wuq_ref[...])
        kvf = _dot(ckv.astype(BF16), wukv_ref[...])
        for hh in range(MLA_HEADS):
            slab = qf[:, hh * MLA_SLAB:(hh + 1) * MLA_SLAB]
            ssq = jnp.sum(jnp.where(lane < MLA_QK_DIM, slab * slab, 0.0), axis=-1, keepdims=True)
            n = slab * lax.rsqrt(ssq / MLA_QK_DIM + EPS) * wq_ref[...]
            q_ref[hh, rows, 0:MLA_NOPE] = n[:, :MLA_NOPE].astype(BF16)
            q_ref[hh, rows, MLA_NOPE:MLA_SLAB] = _rope_lanes(n[:, MLA_NOPE:], cos_t, sin_t).astype(BF16)
        rotk = _rope_lanes(kr2 * wkr_ref[...], cos_t, sin_t)
        _head_keys(kvf, kr2, rotk, wkn_ref[...], k_ref, v_ref, rows)


def _mla_call(h, w1, wuq, wukv, qnw, kvnw, wq, wkn, wkr, cos_t, sin_t, layer):
    tm = 512
    n1 = MLA_Q_RANK + MLA_KV_RANK + 2 * MLA_ROPE
    lsel = lambda i: (layer, 0, 0)
    return pl.pallas_call(
        _mla_kernel,
        out_shape=(
            jax.ShapeDtypeStruct((MLA_HEADS, T, MLA_SLAB), BF16),
            jax.ShapeDtypeStruct((MLA_HEADS, T, MLA_SLAB), BF16),
            jax.ShapeDtypeStruct((MLA_HEADS, T, MLA_V), BF16),
            jax.ShapeDtypeStruct((T, MLA_KV_RANK), F32),
            jax.ShapeDtypeStruct((T, MLA_ROPE), F32),
        ),
        grid=(T // tm,),
        in_specs=[
            pl.BlockSpec((tm, D_MODEL), lambda i: (i, 0)),
            pl.BlockSpec((None, D_MODEL, n1), lsel),
            pl.BlockSpec((None, MLA_Q_RANK, MLA_HEADS * MLA_SLAB), lsel),
            pl.BlockSpec((None, MLA_KV_RANK, MLA_HEADS * (MLA_NOPE + MLA_V)), lsel),
            pl.BlockSpec((None, 1, MLA_Q_RANK), lsel),
            pl.BlockSpec((None, 1, MLA_KV_RANK), lsel),
            pl.BlockSpec((None, 1, MLA_SLAB), lsel),
            pl.BlockSpec((None, 1, MLA_NOPE), lsel),
            pl.BlockSpec((None, 1, 2 * MLA_ROPE), lsel),
            pl.BlockSpec((tm, 2 * MLA_ROPE), lambda i: (i, 0)),
            pl.BlockSpec((tm, 2 * MLA_ROPE), lambda i: (i, 0)),
        ],
        out_specs=(
            pl.BlockSpec((MLA_HEADS, tm, MLA_SLAB), lambda i: (0, i, 0)),
            pl.BlockSpec((MLA_HEADS, tm, MLA_SLAB), lambda i: (0, i, 0)),
            pl.BlockSpec((MLA_HEADS, tm, MLA_V), lambda i: (0, i, 0)),
            pl.BlockSpec((tm, MLA_KV_RANK), lambda i: (i, 0)),
            pl.BlockSpec((tm, MLA_ROPE), lambda i: (i, 0)),
        ),
        compiler_params=_cparams(("parallel",), 48),
        name="mla_proj",
    )(h, w1, wuq, wukv, qnw, kvnw, wq, wkn, wkr, cos_t, sin_t)


def _ctxkv_kernel(c_ref, kr_ref, wukv_ref, wkn_ref, wkr_ref, k_ref, v_ref):
    kvf = _dot(c_ref[...].astype(BF16), wukv_ref[...])
    kr2 = kr_ref[...]
    _head_keys(kvf, kr2, kr2 * wkr_ref[...], wkn_ref[...], k_ref, v_ref, slice(None))


def _ctxkv_call(cache_ckv, cache_kr_pad, wukv, wkn, wkr):
    lsel = lambda l, b: (l, 0, 0)
    return pl.pallas_call(
        _ctxkv_kernel,
        out_shape=(
            jax.ShapeDtypeStruct((DEPTH, MLA_HEADS, DEC_BATCH * PAST_LEN, MLA_SLAB), BF16),
            jax.ShapeDtypeStruct((DEPTH, MLA_HEADS, DEC_BATCH * PAST_LEN, MLA_V), BF16),
        ),
        grid=(DEPTH, DEC_BATCH),
        in_specs=[
            pl.BlockSpec((None, None, PAST_LEN, MLA_KV_RANK), lambda l, b: (b, l, 0, 0)),
            pl.BlockSpec((None, None, PAST_LEN, 2 * MLA_ROPE), lambda l, b: (b, l, 0, 0)),
            pl.BlockSpec((None, MLA_KV_RANK, MLA_HEADS * (MLA_NOPE + MLA_V)), lsel),
            pl.BlockSpec((None, 1, MLA_NOPE), lsel),
            pl.BlockSpec((None, 1, 2 * MLA_ROPE), lsel),
        ],
        out_specs=(
            pl.BlockSpec((None, MLA_HEADS, PAST_LEN, MLA_SLAB), lambda l, b: (l, 0, b, 0)),
            pl.BlockSpec((None, MLA_HEADS, PAST_LEN, MLA_V), lambda l, b: (l, 0, b, 0)),
        ),
        compiler_params=_cparams(("parallel", "parallel"), 32),
        name="ctx_kv",
    )(cache_ckv, cache_kr_pad, wukv, wkn, wkr)


def _attend(q, keys, values):
    scores = [_dot_nt(q, k) for k in keys]
    m = functools.reduce(jnp.maximum, [jnp.max(s, axis=-1, keepdims=True) for s in scores])
    probs = [jnp.exp(s - m) for s in scores]
    denom = sum(jnp.sum(p, axis=-1, keepdims=True) for p in probs)
    o = sum(_dot(p.astype(BF16), v) for p, v in zip(probs, values))
    return (o / denom).astype(BF16)


ATTN_ROWS = 256


def _attn_prompt_kernel(q_ref, k_ref, v_ref, _, o_ref):
    for hh in range(MLA_HEADS):
        o_ref[:, hh * MLA_V:(hh + 1) * MLA_V] = _attend(q_ref[hh], [k_ref[hh]], [v_ref[hh]])


def _attn_sample_kernel(q_ref, ks_ref, vs_ref, kc_ref, vc_ref, _, o_ref):
    for r in range(q_ref.shape[0] // ATTN_ROWS):
        rows = slice(r * ATTN_ROWS, (r + 1) * ATTN_ROWS)
        o_ref[rows, :] = _attend(q_ref[rows, :], [ks_ref[...], kc_ref[...]], [vs_ref[...], vc_ref[...]])


def _attn_prompt_call(q, k, v, y_prev):
    return pl.pallas_call(
        _attn_prompt_kernel,
        out_shape=jax.ShapeDtypeStruct((T, MLA_HEADS * MLA_V), BF16),
        grid=(BATCH,),
        in_specs=[
            pl.BlockSpec((MLA_HEADS, SEQ, MLA_SLAB), lambda b: (0, b, 0)),
            pl.BlockSpec((MLA_HEADS, SEQ, MLA_SLAB), lambda b: (0, b, 0)),
            pl.BlockSpec((MLA_HEADS, SEQ, MLA_V), lambda b: (0, b, 0)),
            pl.BlockSpec(memory_space=pl.ANY),
        ],
        out_specs=pl.BlockSpec((SEQ, MLA_HEADS * MLA_V), lambda b: (b, 0)),
        input_output_aliases={3: 0},
        compiler_params=_cparams(("parallel",), 32),
        name="attn_prompt",
    )(q, k, v, y_prev)


def _attn_sample_call(q, k, v, kc, vc, y_prev, layer):
    tq = 1024
    nq = DEC_SEQ // tq
    q0 = T_P // tq
    s0 = T_P // DEC_SEQ
    return pl.pallas_call(
        _attn_sample_kernel,
        out_shape=jax.ShapeDtypeStruct((T, MLA_HEADS * MLA_V), BF16),
        grid=(DEC_BATCH, MLA_HEADS, nq),
        in_specs=[
            pl.BlockSpec((None, tq, MLA_SLAB), lambda b, h, i: (h, q0 + b * nq + i, 0)),
            pl.BlockSpec((None, DEC_SEQ, MLA_SLAB), lambda b, h, i: (h, s0 + b, 0)),
            pl.BlockSpec((None, DEC_SEQ, MLA_V), lambda b, h, i: (h, s0 + b, 0)),
            pl.BlockSpec((None, None, PAST_LEN, MLA_SLAB), lambda b, h, i: (layer, h, b, 0)),
            pl.BlockSpec((None, None, PAST_LEN, MLA_V), lambda b, h, i: (layer, h, b, 0)),
            pl.BlockSpec(memory_space=pl.ANY),
        ],
        out_specs=pl.BlockSpec((tq, MLA_V), lambda b, h, i: (q0 + b * nq + i, h)),
        input_output_aliases={5: 0},
        compiler_params=_cparams(("parallel", "parallel", "parallel"), 48),
        name="attn_sample",
    )(q, k, v, kc, vc, y_prev)


def _merge_kernel(h_ref, ya_ref, yb_ref, yc_ref, wga_ref, wgb_ref, wgc_ref, wb_ref, o_ref):
    h = h_ref[...]
    m = jax.nn.sigmoid(_dot(h, wga_ref[...])) * _dot(ya_ref[...], wb_ref[0])
    m = m + jax.nn.sigmoid(_dot(h, wgb_ref[...])) * _dot(yb_ref[...], wb_ref[1])
    m = m + jax.nn.sigmoid(_dot(h, wgc_ref[...])) * _dot(yc_ref[...], wb_ref[2])
    o_ref[...] = m.astype(BF16)


def _merge_call(h, ya, yb, yc, w_g, w_b, layer):
    tm, tn = 1024, 512
    nj = D_MODEL // tn
    yspec = pl.BlockSpec((tm, BRANCH_WIDTH), lambda i, j: (i, 0))
    return pl.pallas_call(
        _merge_kernel,
        out_shape=jax.ShapeDtypeStruct((T, D_MODEL), BF16),
        grid=(T // tm, nj),
        in_specs=[
            pl.BlockSpec((tm, D_MODEL), lambda i, j: (i, 0)),
            yspec, yspec, yspec,
            pl.BlockSpec((None, D_MODEL, tn), lambda i, j: (layer, 0, j)),
            pl.BlockSpec((None, D_MODEL, tn), lambda i, j: (layer, 0, j + nj)),
            pl.BlockSpec((None, D_MODEL, tn), lambda i, j: (layer, 0, j + 2 * nj)),
            pl.BlockSpec((None, 3, BRANCH_WIDTH, tn), lambda i, j: (layer, 0, 0, j)),
        ],
        out_specs=pl.BlockSpec((tm, tn), lambda i, j: (i, j)),
        compiler_params=_cparams(("parallel", "parallel"), 56),
        name="gated_merge",
    )(h, ya, yb, yc, w_g, w_g, w_g, w_b)


def _outproj_kernel(m_ref, w_ref, x_ref, g_ref, sh_ref, sc_ref, xo_ref, ho_ref):
    xn = x_ref[...] + g_ref[...] * _dot(m_ref[...], w_ref[...])
    xo_ref[...] = xn
    ho_ref[...] = _norm_mod(xn, sh_ref[...], sc_ref[...]).astype(BF16)


def _outproj_call(m, w_out, x, mod5, layer):
    tm = 512
    return pl.pallas_call(
        _outproj_kernel,
        out_shape=(jax.ShapeDtypeStruct((T, D_MODEL), F32), jax.ShapeDtypeStruct((T, D_MODEL), BF16)),
        grid=(T // tm,),
        in_specs=[
            pl.BlockSpec((tm, D_MODEL), lambda i: (i, 0)),
            pl.BlockSpec((None, D_MODEL, D_MODEL), lambda i: (layer, 0, 0)),
            pl.BlockSpec((tm, D_MODEL), lambda i: (i, 0)),
            _mod_spec(layer, 5, tm),
            _mod_spec(layer, 6, tm),
            _mod_spec(layer, 7, tm),
        ],
        out_specs=(pl.BlockSpec((tm, D_MODEL), lambda i: (i, 0)),
                   pl.BlockSpec((tm, D_MODEL), lambda i: (i, 0))),
        compiler_params=_cparams(("parallel",), 52),
        name="out_proj",
    )(m, w_out, x, mod5, mod5, mod5)


def _swap_halves(x):
    half = x.shape[-1] // 2
    return jnp.concatenate([x[..., half:], x[..., :half]], axis=-1)


def _rope_tables():
    t = jnp.arange(DEC_SEQ)
    row = (t // GRID_W).astype(F32)
    col = (t % GRID_W).astype(F32)
    n_freq = MLA_ROPE // 4
    freqs = jnp.power(ROPE_THETA, -jnp.arange(n_freq, dtype=F32) / n_freq)
    ang = jnp.concatenate([row[:, None] * freqs, col[:, None] * freqs], axis=-1)
    cos, sin = jnp.cos(ang), jnp.sin(ang)
    zeros = jnp.zeros((DEC_SEQ, MLA_ROPE), F32)
    cos_s = jnp.tile(jnp.concatenate([cos, cos, zeros], axis=-1), (DEC_BATCH, 1))
    sin_s = jnp.tile(jnp.concatenate([-sin, sin, zeros], axis=-1), (DEC_BATCH, 1))
    cos_p = jnp.concatenate([jnp.ones((T_P, MLA_ROPE), F32), jnp.zeros((T_P, MLA_ROPE), F32)], axis=-1)
    sin_p = jnp.zeros((T_P, 2 * MLA_ROPE), F32)
    return jnp.concatenate([cos_p, cos_s], axis=0), jnp.concatenate([sin_p, sin_s], axis=0)


def kernel(x_prompt, x_sample, cache_mla_ckv, cache_mla_krope, state_retention, c, c_ctx, w_mod, b_mod,
           ffn1_w_gate_up, ffn1_w_down, ffn2_w_gate_up, ffn2_w_down, w_in, cm_norm_w, cm_w_spatial,
           cm_b_spatial, ret_decay_logit, ret_norm_w, mla_q_norm_w, mla_kv_norm_w, mla_w_uq, mla_w_ukv,
           qk_norm_q_w, qk_norm_k_w, w_branch, w_out):
    w_main = w_in.astype(BF16)
    w_gate = w_main[:, :, W_GATE_COL:]
    w_kr = w_main[:, :, W_KR_COL:W_GATE_COL]
    w_mla1 = jnp.concatenate([w_main[:, :, W_MLA_COL:W_KR_COL], w_kr, _swap_halves(w_kr)], axis=-1)
    ffn1_gu = ffn1_w_gate_up
    ffn1_d = ffn1_w_down.astype(BF16)
    ffn2_gu = ffn2_w_gate_up
    ffn2_d = ffn2_w_down.astype(BF16)
    w_b = w_branch.astype(BF16)
    w_o = w_out.astype(BF16)

    uq = mla_w_uq.reshape(DEPTH, MLA_Q_RANK, MLA_HEADS, MLA_QK_DIM)
    uq_rope = uq[..., MLA_NOPE:]
    wuq = jnp.concatenate([uq[..., :MLA_NOPE], uq_rope, _swap_halves(uq_rope)], axis=-1)
    wuq = wuq.reshape(DEPTH, MLA_Q_RANK, MLA_HEADS * MLA_SLAB).astype(BF16)
    ukv = mla_w_ukv.reshape(DEPTH, MLA_KV_RANK, MLA_HEADS, MLA_NOPE + MLA_V)
    wukv = jnp.concatenate([ukv[..., :MLA_NOPE].reshape(DEPTH, MLA_KV_RANK, MLA_HEADS * MLA_NOPE),
                            ukv[..., MLA_NOPE:].reshape(DEPTH, MLA_KV_RANK, MLA_HEADS * MLA_V)],
                           axis=-1).astype(BF16)
    q_rope_w = qk_norm_q_w[:, MLA_NOPE:]
    wq = (jnp.concatenate([qk_norm_q_w[:, :MLA_NOPE], q_rope_w, _swap_halves(q_rope_w)], axis=-1)
          * (MLA_QK_DIM ** -0.5)).reshape(DEPTH, 1, MLA_SLAB)
    k_rope_w = qk_norm_k_w[:, MLA_NOPE:]
    wkn = qk_norm_k_w[:, :MLA_NOPE].reshape(DEPTH, 1, MLA_NOPE)
    wkr = jnp.concatenate([k_rope_w, _swap_halves(k_rope_w)], axis=-1).reshape(DEPTH, 1, 2 * MLA_ROPE)
    qnw = mla_q_norm_w.reshape(DEPTH, 1, MLA_Q_RANK)
    kvnw = mla_kv_norm_w.reshape(DEPTH, 1, MLA_KV_RANK)
    cm_nw = cm_norm_w.reshape(DEPTH, 1, CM_WIDTH)
    wsp = cm_w_spatial.astype(BF16)
    bsp = jnp.broadcast_to(cm_b_spatial[..., None], (DEPTH, CM_GROUPS, CHUNK, CHUNK))
    ret_nw = ret_norm_w.reshape(DEPTH, 1, RET_V_WIDTH)
    lg = jax.nn.log_sigmoid(ret_decay_logit.astype(F32)).reshape(-1)
    ret_scale = jnp.concatenate([jnp.ones((RET_QK_WIDTH,), F32), jnp.full((RET_QK_WIDTH,), RET_DK ** -0.5, F32),
                                 jnp.ones((2 * RET_V_WIDTH,), F32)]).reshape(1, -1)
    cos_t, sin_t = _rope_tables()
    cache_kr_pad = jnp.pad(cache_mla_krope, ((0, 0), (0, 0), (0, 0), (0, MLA_ROPE)))

    cond = jnp.concatenate([c_ctx[None, :], c, jnp.zeros((MOD_ROWS - 1 - DEC_BATCH, D_MODEL), F32)], axis=0)
    x = jnp.concatenate([x_prompt.reshape(T_P, D_MODEL), x_sample.reshape(T_S, D_MODEL)], axis=0)

    mod5 = _mod_call(cond, w_mod, b_mod).reshape(DEPTH, MOD_ROWS, N_MOD, 1, D_MODEL)
    kc, vc = _ctxkv_call(cache_mla_ckv, cache_kr_pad, wukv, wkn, wkr)
    h = _prenorm_call(x, mod5)

    ckv_layers, krope_layers, ret_layers = [], [], []
    for l in range(DEPTH):
        x, h = _ffn_call(h, x, ffn1_gu, ffn1_d, mod5, l, 2, l, 3)

        ya = _cm_call(h, w_main, cm_nw, wsp, bsp, l)

        rp = _retproj_call(h, w_main, ret_scale, l)
        yb_init = jnp.zeros((T, RET_V_WIDTH), BF16)
        yb, ret_state = _ret_call(lg, rp, ret_nw, None, yb_init, l, prompt=True)
        (yb,) = _ret_call(lg, rp, ret_nw, state_retention, yb, l, prompt=False)

        q, k, v, ckv, kr = _mla_call(h, w_mla1, wuq, wukv, qnw, kvnw, wq, wkn, wkr, cos_t, sin_t, l)
        yc_init = jnp.zeros((T, MLA_HEADS * MLA_V), BF16)
        yc = _attn_prompt_call(q, k, v, yc_init)
        yc = _attn_sample_call(q, k, v, kc, vc, yc, l)

        merged = _merge_call(h, ya, yb, yc, w_gate, w_b, l)
        x, h = _outproj_call(merged, w_o, x, mod5, l)

        nl = min(l + 1, DEPTH - 1)
        x, h = _ffn_call(h, x, ffn2_gu, ffn2_d, mod5, l, 8, nl, 0)

        ckv_layers.append(ckv[:T_P].reshape(BATCH, SEQ, MLA_KV_RANK))
        krope_layers.append(kr[:T_P].reshape(BATCH, SEQ, MLA_ROPE))
        ret_layers.append(ret_state)

    y_prompt = x[:T_P].reshape(BATCH, SEQ, D_MODEL)
    y_sample = x[T_P:].reshape(DEC_BATCH, DEC_SEQ, D_MODEL)
    return (y_prompt, y_sample, jnp.stack(ckv_layers, axis=1), jnp.stack(krope_layers, axis=1),
            jnp.stack(ret_layers, axis=1))
```

```python
import functools

import jax
import jax.numpy as jnp
from jax import lax
from jax.experimental import pallas as pl
from jax.experimental.pallas import tpu as pltpu

F32 = jnp.float32
BF16 = jnp.bfloat16

D_MODEL = 2048
BATCH = 16
SEQ = 256
DEPTH = 4
DEC_BATCH = 8
DEC_SEQ = 2048
PAST_LEN = 512
GRID_W = 64
CHUNK = 128
D_FF = 5632
N_MOD = 9
EPS = 1e-6
ROPE_THETA = 10000.0
CM_GROUPS = 8
CM_WIDTH = 1024
RET_HEADS = 4
RET_DK = 128
RET_DV = 256
RET_QK_WIDTH = RET_HEADS * RET_DK
RET_V_WIDTH = RET_HEADS * RET_DV
MLA_HEADS = 8
MLA_NOPE = 128
MLA_ROPE = 64
MLA_V = 128
MLA_Q_RANK = 512
MLA_KV_RANK = 512
MLA_QK_DIM = MLA_NOPE + MLA_ROPE
MLA_SLAB = 256
BRANCH_WIDTH = 1024

W_RET_COL = 2 * CM_WIDTH
W_MLA_COL = W_RET_COL + 2 * RET_QK_WIDTH + 2 * RET_V_WIDTH
W_KR_COL = W_MLA_COL + MLA_Q_RANK + MLA_KV_RANK
W_GATE_COL = W_KR_COL + MLA_ROPE

T_P = BATCH * SEQ
T_S = DEC_BATCH * DEC_SEQ
T = T_P + T_S
MOD_ROWS = 16

MIB = 1024 * 1024
MXU_DEPTH = 256


def _cparams(semantics, vmem_mib):
    return pltpu.CompilerParams(dimension_semantics=semantics, vmem_limit_bytes=vmem_mib * MIB)


def _mod_row(i, tm):
    n_p = T_P // tm
    return jnp.where(i < n_p, 0, 1 + (i - n_p) // (DEC_SEQ // tm))


def _mod_spec(layer, chunk, tm):
    return pl.BlockSpec((None, None, None, 1, D_MODEL),
                        lambda i, *_: (layer, _mod_row(i, tm), chunk, 0, 0))


def _rms(x):
    return x * lax.rsqrt(jnp.mean(x * x, axis=-1, keepdims=True) + EPS)


def _norm_mod(x, sh, sc):
    return _rms(x) * (1 + sc) + sh


def _dot(a, b):
    return jnp.dot(a, b, preferred_element_type=F32)


def _dot_nt(a, b):
    return lax.dot_general(a, b, (((1,), (1,)), ((), ())), preferred_element_type=F32)


def _dot_tn(a, b):
    return lax.dot_general(a, b, (((0,), (0,)), ((), ())), preferred_element_type=F32)


def _mod_kernel(c_ref, w_ref, b_ref, o_ref):
    s = jax.nn.silu(c_ref[...]).astype(BF16)
    o_ref[...] = _dot(s, w_ref[...].astype(BF16)) + b_ref[...]


def _mod_call(cond, w_mod, b_mod):
    tn = 1024
    n = N_MOD * D_MODEL
    return pl.pallas_call(
        _mod_kernel,
        out_shape=jax.ShapeDtypeStruct((DEPTH, MOD_ROWS, n), F32),
        grid=(DEPTH, n // tn),
        in_specs=[
            pl.BlockSpec((MOD_ROWS, D_MODEL), lambda l, j: (0, 0)),
            pl.BlockSpec((None, D_MODEL, tn), lambda l, j: (l, 0, j)),
            pl.BlockSpec((None, 1, tn), lambda l, j: (l, 0, j)),
        ],
        out_specs=pl.BlockSpec((None, MOD_ROWS, tn), lambda l, j: (l, 0, j)),
        compiler_params=_cparams(("parallel", "parallel"), 40),
        name="mod",
    )(cond, w_mod, b_mod.reshape(DEPTH, 1, n))


def _prenorm_kernel(x_ref, sh_ref, sc_ref, h_ref):
    h_ref[...] = _norm_mod(x_ref[...], sh_ref[...], sc_ref[...]).astype(BF16)


def _prenorm_call(x, mod5):
    tm = 512
    return pl.pallas_call(
        _prenorm_kernel,
        out_shape=jax.ShapeDtypeStruct((T, D_MODEL), BF16),
        grid=(T // tm,),
        in_specs=[
            pl.BlockSpec((tm, D_MODEL), lambda i: (i, 0)),
            _mod_spec(0, 0, tm),
            _mod_spec(0, 1, tm),
        ],
        out_specs=pl.BlockSpec((tm, D_MODEL), lambda i: (i, 0)),
        compiler_params=_cparams(("parallel",), 32),
        name="prenorm",
    )(x, mod5, mod5)


def _ffn_up_kernel(h_ref, wg_ref, wu_ref, a_ref):
    wg = wg_ref[...].astype(BF16)
    wu = wu_ref[...].astype(BF16)
    sub = h_ref.shape[0] // 2
    for s in range(2):
        rows = slice(s * sub, (s + 1) * sub)
        h = h_ref[rows, :]
        a_ref[rows, :] = (jax.nn.silu(_dot(h, wg)) * _dot(h, wu)).astype(BF16)


def _ffn_down_kernel(a_ref, at_ref, wd_ref, wdt_ref, xs_ref, g_ref, sh_ref, sc_ref, xo_ref, ho_ref, *, nk):
    k = pl.program_id(1)
    tn = D_MODEL // nk
    half_g = 0.5 * g_ref[...]
    chunks = [slice(n * tn, (n + 1) * tn) for n in range(nk)]

    def step_product(cols, with_tail=False):
        d = _dot(a_ref[...], wd_ref[:, cols])
        if with_tail:
            d = d + _dot(at_ref[...], wdt_ref[:, cols])
        return half_g[:, cols] * d

    @pl.when(k == 0)
    def _():
        xo_ref[...] = jnp.zeros_like(xo_ref)

    @pl.when(k < nk - 1)
    def _():
        for cols in chunks:
            xo_ref[:, cols] += step_product(cols)
        for n in range(nk - 1):
            @pl.when(k == n)
            def _(n=n):
                xo_ref[:, chunks[n]] += xs_ref[...]

    @pl.when(k == nk - 1)
    def _():
        ssq = jnp.zeros((xo_ref.shape[0], 1), F32)
        for n, cols in enumerate(chunks):
            val = xo_ref[:, cols] + step_product(cols, with_tail=True)
            if n == nk - 1:
                val = val + xs_ref[...]
            xo_ref[:, cols] = val
            ssq = ssq + jnp.sum(val * val, axis=-1, keepdims=True)
        r = lax.rsqrt(ssq / D_MODEL + EPS)
        for cols in chunks:
            y = xo_ref[:, cols] * r * (1 + sc_ref[:, cols]) + sh_ref[:, cols]
            ho_ref[:, cols] = y.astype(BF16)


def _ffn_call(h, x, w_gu, w_d, mod5, layer, gate_chunk, next_layer, next_chunk):
    tm, tf = 2048, 512
    nf = D_FF // tf
    a = pl.pallas_call(
        _ffn_up_kernel,
        out_shape=jax.ShapeDtypeStruct((T, D_FF), BF16),
        grid=(T // tm, nf),
        in_specs=[
            pl.BlockSpec((tm, D_MODEL), lambda i, j: (i, 0)),
            pl.BlockSpec((None, D_MODEL, tf), lambda i, j: (layer, 0, j)),
            pl.BlockSpec((None, D_MODEL, tf), lambda i, j: (layer, 0, j + nf)),
        ],
        out_specs=pl.BlockSpec((tm, tf), lambda i, j: (i, j)),
        compiler_params=_cparams(("parallel", "parallel"), 52),
        name="ffn_up",
    )(h, w_gu, w_gu)
    tm, nk = 1024, 4
    tk = (D_FF // MXU_DEPTH // nk) * MXU_DEPTH
    tail = D_FF - nk * tk
    assert tail > 0 and (nk * tk) % tail == 0
    tail_blk = nk * tk // tail
    return pl.pallas_call(
        functools.partial(_ffn_down_kernel, nk=nk),
        out_shape=(jax.ShapeDtypeStruct((T, D_MODEL), F32), jax.ShapeDtypeStruct((T, D_MODEL), BF16)),
        grid=(T // tm, nk),
        in_specs=[
            pl.BlockSpec((tm, tk), lambda i, k: (i, k)),
            pl.BlockSpec((tm, tail), lambda i, k: (i, tail_blk)),
            pl.BlockSpec((None, tk, D_MODEL), lambda i, k: (layer, k, 0)),
            pl.BlockSpec((None, tail, D_MODEL), lambda i, k: (layer, tail_blk, 0)),
            pl.BlockSpec((tm, D_MODEL // nk), lambda i, k: (i, k)),
            _mod_spec(layer, gate_chunk, tm),
            _mod_spec(next_layer, next_chunk, tm),
            _mod_spec(next_layer, next_chunk + 1, tm),
        ],
        out_specs=(pl.BlockSpec((tm, D_MODEL), lambda i, k: (i, 0)),
                   pl.BlockSpec((tm, D_MODEL), lambda i, k: (i, 0))),
        compiler_params=_cparams(("parallel", "arbitrary"), 58),
        name="ffn_down",
    )(a, a, w_d, w_d, x, mod5, mod5, mod5)


def _cm_kernel(h_ref, w_ref, nw_ref, wsp_ref, bsp_ref, y_ref):
    p = _dot(h_ref[...], w_ref[...])
    u = jax.nn.gelu(p[:, :CM_WIDTH])
    v = jax.nn.gelu(p[:, CM_WIDTH:])
    vn = (_rms(v) * nw_ref[...]).astype(BF16)
    for c in range(h_ref.shape[0] // CHUNK):
        rows = slice(c * CHUNK, (c + 1) * CHUNK)
        for g in range(CM_GROUPS):
            cols = slice(g * CHUNK, (g + 1) * CHUNK)
            mixed = _dot(wsp_ref[g], vn[rows, cols]) + bsp_ref[g]
            y_ref[rows, cols] = (u[rows, cols] * mixed).astype(BF16)


def _cm_call(h, w_cm, cm_nw, wsp, bsp, layer):
    tm = 512
    return pl.pallas_call(
        _cm_kernel,
        out_shape=jax.ShapeDtypeStruct((T, CM_WIDTH), BF16),
        grid=(T // tm,),
        in_specs=[
            pl.BlockSpec((tm, D_MODEL), lambda i: (i, 0)),
            pl.BlockSpec((None, D_MODEL, 2 * CM_WIDTH), lambda i: (layer, 0, 0)),
            pl.BlockSpec((None, 1, CM_WIDTH), lambda i: (layer, 0, 0)),
            pl.BlockSpec((None, CM_GROUPS, CHUNK, CHUNK), lambda i: (layer, 0, 0, 0)),
            pl.BlockSpec((None, CM_GROUPS, CHUNK, CHUNK), lambda i: (layer, 0, 0, 0)),
        ],
        out_specs=pl.BlockSpec((tm, CM_WIDTH), lambda i: (i, 0)),
        compiler_params=_cparams(("parallel",), 48),
        name="chunk_mlp",
    )(h, w_cm, cm_nw, wsp, bsp)


def _retproj_kernel(h_ref, w_ref, s_ref, o_ref):
    o_ref[...] = (_dot(h_ref[...], w_ref[...]) * s_ref[...]).astype(BF16)


def _retproj_call(h, w_ret, col_scale, layer):
    tm, tn = 1024, 1024
    n = 2 * RET_QK_WIDTH + 2 * RET_V_WIDTH
    return pl.pallas_call(
        _retproj_kernel,
        out_shape=jax.ShapeDtypeStruct((T, n), BF16),
        grid=(T // tm, n // tn),
        in_specs=[
            pl.BlockSpec((tm, D_MODEL), lambda i, j: (i, 0)),
            pl.BlockSpec((None, D_MODEL, tn), lambda i, j: (layer, 0, W_RET_COL // tn + j)),
            pl.BlockSpec((1, tn), lambda i, j: (0, j)),
        ],
        out_specs=pl.BlockSpec((tm, tn), lambda i, j: (i, j)),
        compiler_params=_cparams(("parallel", "parallel"), 40),
        name="ret_proj",
    )(h, w_ret, col_scale)


def _ret_kernel(*refs, layer, n_chunks, has_s0, emit_state):
    refs = list(refs)
    lg_ref, q_ref, k_ref, v_ref, g_ref, nw_ref = refs[:6]
    pos = 6
    s0_ref = None
    if has_s0:
        s0_ref = refs[pos]
        pos += 1
    pos += 1
    y_ref = refs[pos]
    pos += 1
    so_ref = None
    if emit_state:
        so_ref = refs[pos]
        pos += 1
    yacc_refs = refs[pos:pos + 2]

    head = pl.program_id(1)
    row = lax.broadcasted_iota(jnp.int32, (CHUNK, CHUNK), 0).astype(F32)
    col = lax.broadcasted_iota(jnp.int32, (CHUNK, CHUNK), 1).astype(F32)
    tables = []
    for d in range(2):
        lg = lg_ref[layer * 2 * RET_HEADS + d * RET_HEADS + head]
        if d == 0:
            rel = row - col
            q_decay = jnp.exp(lg * (row + 1.0))
            k_decay = jnp.exp(lg * (CHUNK - 1.0 - row))
        else:
            rel = col - row
            q_decay = jnp.exp(lg * (CHUNK - row))
            k_decay = jnp.exp(lg * row)
        intra = jnp.where(rel >= 0, jnp.exp(lg * jnp.maximum(rel, 0.0)), 0.0)
        chunk_decay = jnp.exp(jnp.full((RET_DK, RET_DV), lg * CHUNK, F32))
        tables.append((intra, q_decay, k_decay, chunk_decay))

    def body(t, states):
        new_states = []
        for d in range(2):
            intra, q_decay, k_decay, chunk_decay = tables[d]
            c = t if d == 0 else n_chunks - 1 - t
            rows = pl.ds(pl.multiple_of(c * CHUNK, CHUNK), CHUNK)
            qc = q_ref[rows, :]
            kc = k_ref[rows, :]
            vc = v_ref[rows, :]
            scores = (_dot_nt(qc, kc) * intra).astype(BF16)
            qd = (qc.astype(F32) * q_decay).astype(BF16)
            kd = (kc.astype(F32) * k_decay).astype(BF16)
            yacc_refs[d][rows, :] = _dot(scores, vc) + _dot(qd, states[d].astype(BF16))
            new_states.append(chunk_decay * states[d] + _dot_tn(kd, vc))
        return tuple(new_states)

    if has_s0:
        s_init = (s0_ref[0], s0_ref[1])
    else:
        s_init = (jnp.zeros((RET_DK, RET_DV), F32), jnp.zeros((RET_DK, RET_DV), F32))
    s_final = lax.fori_loop(0, n_chunks, body, s_init, unroll=min(4, n_chunks))
    if emit_state:
        so_ref[0] = s_final[0]
        so_ref[1] = s_final[1]

    y = yacc_refs[0][...] + yacc_refs[1][...]
    mu = jnp.mean(y, axis=-1, keepdims=True)
    var = jnp.mean(jnp.square(y - mu), axis=-1, keepdims=True)
    yn = (y - mu) * lax.rsqrt(var + EPS) * nw_ref[...]
    y_ref[...] = (jax.nn.silu(g_ref[...].astype(F32)) * yn).astype(BF16)


def _ret_call(lg, rp, ret_nw, s0, y_prev, layer, *, prompt):
    seq = SEQ if prompt else DEC_SEQ
    nb = BATCH if prompt else DEC_BATCH
    rb0 = 0 if prompt else T_P // DEC_SEQ
    kq = RET_QK_WIDTH // RET_DK
    kv = 2 * RET_QK_WIDTH // RET_DV
    kg = kv + RET_HEADS
    in_specs = [
        pl.BlockSpec(memory_space=pltpu.SMEM),
        pl.BlockSpec((seq, RET_DK), lambda b, h: (rb0 + b, h)),
        pl.BlockSpec((seq, RET_DK), lambda b, h: (rb0 + b, kq + h)),
        pl.BlockSpec((seq, RET_DV), lambda b, h: (rb0 + b, kv + h)),
        pl.BlockSpec((seq, RET_DV), lambda b, h: (rb0 + b, kg + h)),
        pl.BlockSpec((None, 1, RET_DV), lambda b, h: (layer, 0, h)),
    ]
    args = [lg, rp, rp, rp, rp, ret_nw]
    if not prompt:
        in_specs.append(pl.BlockSpec((None, None, 2, None, RET_DK, RET_DV), lambda b, h: (b, layer, 0, h, 0, 0)))
        args.append(s0)
    in_specs.append(pl.BlockSpec(memory_space=pl.ANY))
    args.append(y_prev)
    out_shape = [jax.ShapeDtypeStruct((T, RET_V_WIDTH), BF16)]
    out_specs = [pl.BlockSpec((seq, RET_DV), lambda b, h: (rb0 + b, h))]
    if prompt:
        out_shape.append(jax.ShapeDtypeStruct((BATCH, 2, RET_HEADS, RET_DK, RET_DV), F32))
        out_specs.append(pl.BlockSpec((None, 2, None, RET_DK, RET_DV), lambda b, h: (b, 0, h, 0, 0)))
    return pl.pallas_call(
        functools.partial(_ret_kernel, layer=layer, n_chunks=seq // CHUNK, has_s0=not prompt,
                          emit_state=prompt),
        out_shape=tuple(out_shape),
        grid=(nb, RET_HEADS),
        in_specs=in_specs,
        out_specs=tuple(out_specs),
        scratch_shapes=[pltpu.VMEM((seq, RET_DV), F32), pltpu.VMEM((seq, RET_DV), F32)],
        input_output_aliases={len(args) - 1: 0},
        compiler_params=_cparams(("parallel", "parallel"), 32),
        name="retention_prompt" if prompt else "retention_sample",
    )(*args)


def _rope_lanes(x, cos_t, sin_t):
    return x * cos_t + pltpu.roll(x, 64, 1) * sin_t


def _head_keys(kvf, kr2, rotk, wkn, k_ref, v_ref, rows):
    lane = lax.broadcasted_iota(jnp.int32, (1, 2 * MLA_ROPE), 1)
    ssq_kr = jnp.sum(jnp.where(lane < MLA_ROPE, kr2 * kr2, 0.0), axis=-1, keepdims=True)
    nk = MLA_HEADS * MLA_NOPE
    for hh in range(MLA_HEADS):
        kn = kvf[:, hh * MLA_NOPE:(hh + 1) * MLA_NOPE]
        ssq = jnp.sum(kn * kn, axis=-1, keepdims=True) + ssq_kr
        r = lax.rsqrt(ssq / MLA_QK_DIM + EPS)
        k_ref[hh, rows, 0:MLA_NOPE] = (kn * r * wkn).astype(BF16)
        k_ref[hh, rows, MLA_NOPE:MLA_SLAB] = (rotk * r).astype(BF16)
        v_ref[hh, rows, :] = kvf[:, nk + hh * MLA_V: nk + (hh + 1) * MLA_V].astype(BF16)


MLA_ROWS = 256


def _mla_kernel(h_ref, w1_ref, wuq_ref, wukv_ref, qnw_ref, kvnw_ref, wq_ref, wkn_ref, wkr_ref, cos_ref, sin_ref,
                q_ref, k_ref, v_ref, ckv_ref, kr_ref):
    lane = lax.broadcasted_iota(jnp.int32, (1, MLA_SLAB), 1)
    for c in range(h_ref.shape[0] // MLA_ROWS):
        rows = slice(c * MLA_ROWS, (c + 1) * MLA_ROWS)
        p = _dot(h_ref[rows, :], w1_ref[...])
        cq = _rms(p[:, :MLA_Q_RANK]) * qnw_ref[...]
        ckv = _rms(p[:, MLA_Q_RANK:MLA_Q_RANK + MLA_KV_RANK]) * kvnw_ref[...]
        kr2 = p[:, MLA_Q_RANK + MLA_KV_RANK:]
        ckv_ref[rows, :] = ckv
        kr_ref[rows, :] = kr2[:, :MLA_ROPE]
        cos_t = cos_ref[rows, :]
        sin_t = sin_ref[rows, :]
        qf = _dot(cq.astype(BF16), wuq_ref[...])
        kvf = _dot(ckv.astype(BF16), wukv_ref[...])
        for hh in range(MLA_HEADS):
            slab = qf[:, hh * MLA_SLAB:(hh + 1) * MLA_SLAB]
            ssq = jnp.sum(jnp.where(lane < MLA_QK_DIM, slab * slab, 0.0), axis=-1, keepdims=True)
            n = slab * lax.rsqrt(ssq / MLA_QK_DIM + EPS) * wq_ref[...]
            q_ref[hh, rows, 0:MLA_NOPE] = n[:, :MLA_NOPE].astype(BF16)
            q_ref[hh, rows, MLA_NOPE:MLA_SLAB] = _rope_lanes(n[:, MLA_NOPE:], cos_t, sin_t).astype(BF16)
        rotk = _rope_lanes(kr2 * wkr_ref[...], cos_t, sin_t)
        _head_keys(kvf, kr2, rotk, wkn_ref[...], k_ref, v_ref, rows)


def _mla_call(h, w1, wuq, wukv, qnw, kvnw, wq, wkn, wkr, cos_t, sin_t, layer):
    tm = 512
    n1 = MLA_Q_RANK + MLA_KV_RANK + 2 * MLA_ROPE
    lsel = lambda i: (layer, 0, 0)
    return pl.pallas_call(
        _mla_kernel,
        out_shape=(
            jax.ShapeDtypeStruct((MLA_HEADS, T, MLA_SLAB), BF16),
            jax.ShapeDtypeStruct((MLA_HEADS, T, MLA_SLAB), BF16),
            jax.ShapeDtypeStruct((MLA_HEADS, T, MLA_V), BF16),
            jax.ShapeDtypeStruct((T, MLA_KV_RANK), F32),
            jax.ShapeDtypeStruct((T, MLA_ROPE), F32),
        ),
        grid=(T // tm,),
        in_specs=[
            pl.BlockSpec((tm, D_MODEL), lambda i: (i, 0)),
            pl.BlockSpec((None, D_MODEL, n1), lsel),
            pl.BlockSpec((None, MLA_Q_RANK, MLA_HEADS * MLA_SLAB), lsel),
            pl.BlockSpec((None, MLA_KV_RANK, MLA_HEADS * (MLA_NOPE + MLA_V)), lsel),
            pl.BlockSpec((None, 1, MLA_Q_RANK), lsel),
            pl.BlockSpec((None, 1, MLA_KV_RANK), lsel),
            pl.BlockSpec((None, 1, MLA_SLAB), lsel),
            pl.BlockSpec((None, 1, MLA_NOPE), lsel),
            pl.BlockSpec((None, 1, 2 * MLA_ROPE), lsel),
            pl.BlockSpec((tm, 2 * MLA_ROPE), lambda i: (i, 0)),
            pl.BlockSpec((tm, 2 * MLA_ROPE), lambda i: (i, 0)),
        ],
        out_specs=(
            pl.BlockSpec((MLA_HEADS, tm, MLA_SLAB), lambda i: (0, i, 0)),
            pl.BlockSpec((MLA_HEADS, tm, MLA_SLAB), lambda i: (0, i, 0)),
            pl.BlockSpec((MLA_HEADS, tm, MLA_V), lambda i: (0, i, 0)),
            pl.BlockSpec((tm, MLA_KV_RANK), lambda i: (i, 0)),
            pl.BlockSpec((tm, MLA_ROPE), lambda i: (i, 0)),
        ),
        compiler_params=_cparams(("parallel",), 48),
        name="mla_proj",
    )(h, w1, wuq, wukv, qnw, kvnw, wq, wkn, wkr, cos_t, sin_t)


def _ctxkv_kernel(c_ref, kr_ref, wukv_ref, wkn_ref, wkr_ref, k_ref, v_ref):
    kvf = _dot(c_ref[...].astype(BF16), wukv_ref[...])
    kr2 = kr_ref[...]
    _head_keys(kvf, kr2, kr2 * wkr_ref[...], wkn_ref[...], k_ref, v_ref, slice(None))


def _ctxkv_call(cache_ckv, cache_kr_pad, wukv, wkn, wkr):
    lsel = lambda l, b: (l, 0, 0)
    return pl.pallas_call(
        _ctxkv_kernel,
        out_shape=(
            jax.ShapeDtypeStruct((DEPTH, MLA_HEADS, DEC_BATCH * PAST_LEN, MLA_SLAB), BF16),
            jax.ShapeDtypeStruct((DEPTH, MLA_HEADS, DEC_BATCH * PAST_LEN, MLA_V), BF16),
        ),
        grid=(DEPTH, DEC_BATCH),
        in_specs=[
            pl.BlockSpec((None, None, PAST_LEN, MLA_KV_RANK), lambda l, b: (b, l, 0, 0)),
            pl.BlockSpec((None, None, PAST_LEN, 2 * MLA_ROPE), lambda l, b: (b, l, 0, 0)),
            pl.BlockSpec((None, MLA_KV_RANK, MLA_HEADS * (MLA_NOPE + MLA_V)), lsel),
            pl.BlockSpec((None, 1, MLA_NOPE), lsel),
            pl.BlockSpec((None, 1, 2 * MLA_ROPE), lsel),
        ],
        out_specs=(
            pl.BlockSpec((None, MLA_HEADS, PAST_LEN, MLA_SLAB), lambda l, b: (l, 0, b, 0)),
            pl.BlockSpec((None, MLA_HEADS, PAST_LEN, MLA_V), lambda l, b: (l, 0, b, 0)),
        ),
        compiler_params=_cparams(("parallel", "parallel"), 32),
        name="ctx_kv",
    )(cache_ckv, cache_kr_pad, wukv, wkn, wkr)


def _attend(q, keys, values):
    scores = [_dot_nt(q, k) for k in keys]
    m = functools.reduce(jnp.maximum, [jnp.max(s, axis=-1, keepdims=True) for s in scores])
    probs = [jnp.exp(s - m) for s in scores]
    denom = sum(jnp.sum(p, axis=-1, keepdims=True) for p in probs)
    o = sum(_dot(p.astype(BF16), v) for p, v in zip(probs, values))
    return (o / denom).astype(BF16)


ATTN_ROWS = 256


def _attn_prompt_kernel(q_ref, k_ref, v_ref, _, o_ref):
    for hh in range(MLA_HEADS):
        o_ref[:, hh * MLA_V:(hh + 1) * MLA_V] = _attend(q_ref[hh], [k_ref[hh]], [v_ref[hh]])


def _attn_sample_kernel(q_ref, ks_ref, vs_ref, kc_ref, vc_ref, _, o_ref):
    for r in range(q_ref.shape[0] // ATTN_ROWS):
        rows = slice(r * ATTN_ROWS, (r + 1) * ATTN_ROWS)
        o_ref[rows, :] = _attend(q_ref[rows, :], [ks_ref[...], kc_ref[...]], [vs_ref[...], vc_ref[...]])


def _attn_prompt_call(q, k, v, y_prev):
    return pl.pallas_call(
        _attn_prompt_kernel,
        out_shape=jax.ShapeDtypeStruct((T, MLA_HEADS * MLA_V), BF16),
        grid=(BATCH,),
        in_specs=[
            pl.BlockSpec((MLA_HEADS, SEQ, MLA_SLAB), lambda b: (0, b, 0)),
            pl.BlockSpec((MLA_HEADS, SEQ, MLA_SLAB), lambda b: (0, b, 0)),
            pl.BlockSpec((MLA_HEADS, SEQ, MLA_V), lambda b: (0, b, 0)),
            pl.BlockSpec(memory_space=pl.ANY),
        ],
        out_specs=pl.BlockSpec((SEQ, MLA_HEADS * MLA_V), lambda b: (b, 0)),
        input_output_aliases={3: 0},
        compiler_params=_cparams(("parallel",), 32),
        name="attn_prompt",
    )(q, k, v, y_prev)


def _attn_sample_call(q, k, v, kc, vc, y_prev, layer):
    tq = 2048
    nq = DEC_SEQ // tq
    q0 = T_P // tq
    s0 = T_P // DEC_SEQ
    return pl.pallas_call(
        _attn_sample_kernel,
        out_shape=jax.ShapeDtypeStruct((T, MLA_HEADS * MLA_V), BF16),
        grid=(DEC_BATCH, MLA_HEADS, nq),
        in_specs=[
            pl.BlockSpec((None, tq, MLA_SLAB), lambda b, h, i: (h, q0 + b * nq + i, 0)),
            pl.BlockSpec((None, DEC_SEQ, MLA_SLAB), lambda b, h, i: (h, s0 + b, 0)),
            pl.BlockSpec((None, DEC_SEQ, MLA_V), lambda b, h, i: (h, s0 + b, 0)),
            pl.BlockSpec((None, None, PAST_LEN, MLA_SLAB), lambda b, h, i: (layer, h, b, 0)),
            pl.BlockSpec((None, None, PAST_LEN, MLA_V), lambda b, h, i: (layer, h, b, 0)),
            pl.BlockSpec(memory_space=pl.ANY),
        ],
        out_specs=pl.BlockSpec((tq, MLA_V), lambda b, h, i: (q0 + b * nq + i, h)),
        input_output_aliases={5: 0},
        compiler_params=_cparams(("parallel", "parallel", "parallel"), 48),
        name="attn_sample",
    )(q, k, v, kc, vc, y_prev)


def _merge_kernel(h_ref, ya_ref, yb_ref, yc_ref, wga_ref, wgb_ref, wgc_ref, wb_ref, o_ref):
    h = h_ref[...]
    m = jax.nn.sigmoid(_dot(h, wga_ref[...])) * _dot(ya_ref[...], wb_ref[0])
    m = m + jax.nn.sigmoid(_dot(h, wgb_ref[...])) * _dot(yb_ref[...], wb_ref[1])
    m = m + jax.nn.sigmoid(_dot(h, wgc_ref[...])) * _dot(yc_ref[...], wb_ref[2])
    o_ref[...] = m.astype(BF16)


def _merge_call(h, ya, yb, yc, w_g, w_b, layer):
    tm, tn = 1024, 512
    nj = D_MODEL // tn
    yspec = pl.BlockSpec((tm, BRANCH_WIDTH), lambda i, j: (i, 0))
    return pl.pallas_call(
        _merge_kernel,
        out_shape=jax.ShapeDtypeStruct((T, D_MODEL), BF16),
        grid=(T // tm, nj),
        in_specs=[
            pl.BlockSpec((tm, D_MODEL), lambda i, j: (i, 0)),
            yspec, yspec, yspec,
            pl.BlockSpec((None, D_MODEL, tn), lambda i, j: (layer, 0, j)),
            pl.BlockSpec((None, D_MODEL, tn), lambda i, j: (layer, 0, j + nj)),
            pl.BlockSpec((None, D_MODEL, tn), lambda i, j: (layer, 0, j + 2 * nj)),
            pl.BlockSpec((None, 3, BRANCH_WIDTH, tn), lambda i, j: (layer, 0, 0, j)),
        ],
        out_specs=pl.BlockSpec((tm, tn), lambda i, j: (i, j)),
        compiler_params=_cparams(("parallel", "parallel"), 56),
        name="gated_merge",
    )(h, ya, yb, yc, w_g, w_g, w_g, w_b)


def _outproj_kernel(m_ref, w_ref, x_ref, g_ref, sh_ref, sc_ref, xo_ref, ho_ref):
    xn = x_ref[...] + g_ref[...] * _dot(m_ref[...], w_ref[...])
    xo_ref[...] = xn
    ho_ref[...] = _norm_mod(xn, sh_ref[...], sc_ref[...]).astype(BF16)


def _outproj_call(m, w_out, x, mod5, layer):
    tm = 512
    return pl.pallas_call(
        _outproj_kernel,
        out_shape=(jax.ShapeDtypeStruct((T, D_MODEL), F32), jax.ShapeDtypeStruct((T, D_MODEL), BF16)),
        grid=(T // tm,),
        in_specs=[
            pl.BlockSpec((tm, D_MODEL), lambda i: (i, 0)),
            pl.BlockSpec((None, D_MODEL, D_MODEL), lambda i: (layer, 0, 0)),
            pl.BlockSpec((tm, D_MODEL), lambda i: (i, 0)),
            _mod_spec(layer, 5, tm),
            _mod_spec(layer, 6, tm),
            _mod_spec(layer, 7, tm),
        ],
        out_specs=(pl.BlockSpec((tm, D_MODEL), lambda i: (i, 0)),
                   pl.BlockSpec((tm, D_MODEL), lambda i: (i, 0))),
        compiler_params=_cparams(("parallel",), 52),
        name="out_proj",
    )(m, w_out, x, mod5, mod5, mod5)


def _swap_halves(x):
    half = x.shape[-1] // 2
    return jnp.concatenate([x[..., half:], x[..., :half]], axis=-1)


def _rope_tables():
    t = jnp.arange(DEC_SEQ)
    row = (t // GRID_W).astype(F32)
    col = (t % GRID_W).astype(F32)
    n_freq = MLA_ROPE // 4
    freqs = jnp.power(ROPE_THETA, -jnp.arange(n_freq, dtype=F32) / n_freq)
    ang = jnp.concatenate([row[:, None] * freqs, col[:, None] * freqs], axis=-1)
    cos, sin = jnp.cos(ang), jnp.sin(ang)
    zeros = jnp.zeros((DEC_SEQ, MLA_ROPE), F32)
    cos_s = jnp.tile(jnp.concatenate([cos, cos, zeros], axis=-1), (DEC_BATCH, 1))
    sin_s = jnp.tile(jnp.concatenate([-sin, sin, zeros], axis=-1), (DEC_BATCH, 1))
    cos_p = jnp.concatenate([jnp.ones((T_P, MLA_ROPE), F32), jnp.zeros((T_P, MLA_ROPE), F32)], axis=-1)
    sin_p = jnp.zeros((T_P, 2 * MLA_ROPE), F32)
    return jnp.concatenate([cos_p, cos_s], axis=0), jnp.concatenate([sin_p, sin_s], axis=0)


def kernel(x_prompt, x_sample, cache_mla_ckv, cache_mla_krope, state_retention, c, c_ctx, w_mod, b_mod,
           ffn1_w_gate_up, ffn1_w_down, ffn2_w_gate_up, ffn2_w_down, w_in, cm_norm_w, cm_w_spatial,
           cm_b_spatial, ret_decay_logit, ret_norm_w, mla_q_norm_w, mla_kv_norm_w, mla_w_uq, mla_w_ukv,
           qk_norm_q_w, qk_norm_k_w, w_branch, w_out):
    w_main = w_in.astype(BF16)
    w_gate = w_main[:, :, W_GATE_COL:]
    w_kr = w_main[:, :, W_KR_COL:W_GATE_COL]
    w_mla1 = jnp.concatenate([w_main[:, :, W_MLA_COL:W_KR_COL], w_kr, _swap_halves(w_kr)], axis=-1)
    ffn1_gu = ffn1_w_gate_up
    ffn1_d = ffn1_w_down.astype(BF16)
    ffn2_gu = ffn2_w_gate_up
    ffn2_d = ffn2_w_down.astype(BF16)
    w_b = w_branch.astype(BF16)
    w_o = w_out.astype(BF16)

    uq = mla_w_uq.reshape(DEPTH, MLA_Q_RANK, MLA_HEADS, MLA_QK_DIM)
    uq_rope = uq[..., MLA_NOPE:]
    wuq = jnp.concatenate([uq[..., :MLA_NOPE], uq_rope, _swap_halves(uq_rope)], axis=-1)
    wuq = wuq.reshape(DEPTH, MLA_Q_RANK, MLA_HEADS * MLA_SLAB).astype(BF16)
    ukv = mla_w_ukv.reshape(DEPTH, MLA_KV_RANK, MLA_HEADS, MLA_NOPE + MLA_V)
    wukv = jnp.concatenate([ukv[..., :MLA_NOPE].reshape(DEPTH, MLA_KV_RANK, MLA_HEADS * MLA_NOPE),
                            ukv[..., MLA_NOPE:].reshape(DEPTH, MLA_KV_RANK, MLA_HEADS * MLA_V)],
                           axis=-1).astype(BF16)
    q_rope_w = qk_norm_q_w[:, MLA_NOPE:]
    wq = (jnp.concatenate([qk_norm_q_w[:, :MLA_NOPE], q_rope_w, _swap_halves(q_rope_w)], axis=-1)
          * (MLA_QK_DIM ** -0.5)).reshape(DEPTH, 1, MLA_SLAB)
    k_rope_w = qk_norm_k_w[:, MLA_NOPE:]
    wkn = qk_norm_k_w[:, :MLA_NOPE].reshape(DEPTH, 1, MLA_NOPE)
    wkr = jnp.concatenate([k_rope_w, _swap_halves(k_rope_w)], axis=-1).reshape(DEPTH, 1, 2 * MLA_ROPE)
    qnw = mla_q_norm_w.reshape(DEPTH, 1, MLA_Q_RANK)
    kvnw = mla_kv_norm_w.reshape(DEPTH, 1, MLA_KV_RANK)
    cm_nw = cm_norm_w.reshape(DEPTH, 1, CM_WIDTH)
    wsp = cm_w_spatial.astype(BF16)
    bsp = jnp.broadcast_to(cm_b_spatial[..., None], (DEPTH, CM_GROUPS, CHUNK, CHUNK))
    ret_nw = ret_norm_w.reshape(DEPTH, 1, RET_V_WIDTH)
    lg = jax.nn.log_sigmoid(ret_decay_logit.astype(F32)).reshape(-1)
    ret_scale = jnp.concatenate([jnp.ones((RET_QK_WIDTH,), F32), jnp.full((RET_QK_WIDTH,), RET_DK ** -0.5, F32),
                                 jnp.ones((2 * RET_V_WIDTH,), F32)]).reshape(1, -1)
    cos_t, sin_t = _rope_tables()
    cache_kr_pad = jnp.pad(cache_mla_krope, ((0, 0), (0, 0), (0, 0), (0, MLA_ROPE)))

    cond = jnp.concatenate([c_ctx[None, :], c, jnp.zeros((MOD_ROWS - 1 - DEC_BATCH, D_MODEL), F32)], axis=0)
    x = jnp.concatenate([x_prompt.reshape(T_P, D_MODEL), x_sample.reshape(T_S, D_MODEL)], axis=0)

    mod5 = _mod_call(cond, w_mod, b_mod).reshape(DEPTH, MOD_ROWS, N_MOD, 1, D_MODEL)
    kc, vc = _ctxkv_call(cache_mla_ckv, cache_kr_pad, wukv, wkn, wkr)
    h = _prenorm_call(x, mod5)

    yb = jnp.zeros((T, RET_V_WIDTH), BF16)
    yc = jnp.zeros((T, MLA_HEADS * MLA_V), BF16)
    ckv_layers, krope_layers, ret_layers = [], [], []
    for l in range(DEPTH):
        x, h = _ffn_call(h, x, ffn1_gu, ffn1_d, mod5, l, 2, l, 3)

        ya = _cm_call(h, w_main, cm_nw, wsp, bsp, l)

        rp = _retproj_call(h, w_main, ret_scale, l)
        yb, ret_state = _ret_call(lg, rp, ret_nw, None, yb, l, prompt=True)
        (yb,) = _ret_call(lg, rp, ret_nw, state_retention, yb, l, prompt=False)

        q, k, v, ckv, kr = _mla_call(h, w_mla1, wuq, wukv, qnw, kvnw, wq, wkn, wkr, cos_t, sin_t, l)
        yc = _attn_prompt_call(q, k, v, yc)
        yc = _attn_sample_call(q, k, v, kc, vc, yc, l)

        merged = _merge_call(h, ya, yb, yc, w_gate, w_b, l)
        x, h = _outproj_call(merged, w_o, x, mod5, l)

        nl = min(l + 1, DEPTH - 1)
        x, h = _ffn_call(h, x, ffn2_gu, ffn2_d, mod5, l, 8, nl, 0)

        ckv_layers.append(ckv[:T_P].reshape(BATCH, SEQ, MLA_KV_RANK))
        krope_layers.append(kr[:T_P].reshape(BATCH, SEQ, MLA_ROPE))
        ret_layers.append(ret_state)

    y_prompt = x[:T_P].reshape(BATCH, SEQ, D_MODEL)
    y_sample = x[T_P:].reshape(DEC_BATCH, DEC_SEQ, D_MODEL)
    return (y_prompt, y_sample, jnp.stack(ckv_layers, axis=1), jnp.stack(krope_layers, axis=1),
            jnp.stack(ret_layers, axis=1))
```

```python
import functools

import jax
import jax.numpy as jnp
from jax import lax
from jax.experimental import pallas as pl
from jax.experimental.pallas import tpu as pltpu

F32 = jnp.float32
BF16 = jnp.bfloat16

D_MODEL = 2048
BATCH = 16
SEQ = 256
DEPTH = 4
DEC_BATCH = 8
DEC_SEQ = 2048
PAST_LEN = 512
GRID_W = 64
CHUNK = 128
D_FF = 5632
N_MOD = 9
EPS = 1e-6
ROPE_THETA = 10000.0
CM_GROUPS = 8
CM_WIDTH = 1024
RET_HEADS = 4
RET_DK = 128
RET_DV = 256
RET_QK_WIDTH = RET_HEADS * RET_DK
RET_V_WIDTH = RET_HEADS * RET_DV
MLA_HEADS = 8
MLA_NOPE = 128
MLA_ROPE = 64
MLA_V = 128
MLA_Q_RANK = 512
MLA_KV_RANK = 512
MLA_QK_DIM = MLA_NOPE + MLA_ROPE
MLA_SLAB = 256
BRANCH_WIDTH = 1024

W_RET_COL = 2 * CM_WIDTH
W_MLA_COL = W_RET_COL + 2 * RET_QK_WIDTH + 2 * RET_V_WIDTH
W_KR_COL = W_MLA_COL + MLA_Q_RANK + MLA_KV_RANK
W_GATE_COL = W_KR_COL + MLA_ROPE

T_P = BATCH * SEQ
T_S = DEC_BATCH * DEC_SEQ
T = T_P + T_S
MOD_ROWS = 16

MIB = 1024 * 1024
MXU_DEPTH = 256


def _cparams(semantics, vmem_mib):
    return pltpu.CompilerParams(dimension_semantics=semantics, vmem_limit_bytes=vmem_mib * MIB)


def _mod_row(i, tm):
    n_p = T_P // tm
    return jnp.where(i < n_p, 0, 1 + (i - n_p) // (DEC_SEQ // tm))


def _mod_spec(layer, chunk, tm, tile0=0):
    return pl.BlockSpec((None, None, None, 1, D_MODEL),
                        lambda i, *_: (layer, _mod_row(i + tile0, tm), chunk, 0, 0))


def _rms(x):
    return x * lax.rsqrt(jnp.mean(x * x, axis=-1, keepdims=True) + EPS)


def _norm_mod(x, sh, sc):
    return _rms(x) * (1 + sc) + sh


def _dot(a, b):
    return jnp.dot(a, b, preferred_element_type=F32)


def _dot_nt(a, b):
    return lax.dot_general(a, b, (((1,), (1,)), ((), ())), preferred_element_type=F32)


def _dot_tn(a, b):
    return lax.dot_general(a, b, (((0,), (0,)), ((), ())), preferred_element_type=F32)


def _mod_kernel(c_ref, w_ref, b_ref, o_ref):
    s = jax.nn.silu(c_ref[...]).astype(BF16)
    o_ref[...] = _dot(s, w_ref[...].astype(BF16)) + b_ref[...]


def _mod_call(cond, w_mod, b_mod):
    tn = 1024
    n = N_MOD * D_MODEL
    return pl.pallas_call(
        _mod_kernel,
        out_shape=jax.ShapeDtypeStruct((DEPTH, MOD_ROWS, n), F32),
        grid=(DEPTH, n // tn),
        in_specs=[
            pl.BlockSpec((MOD_ROWS, D_MODEL), lambda l, j: (0, 0)),
            pl.BlockSpec((None, D_MODEL, tn), lambda l, j: (l, 0, j)),
            pl.BlockSpec((None, 1, tn), lambda l, j: (l, 0, j)),
        ],
        out_specs=pl.BlockSpec((None, MOD_ROWS, tn), lambda l, j: (l, 0, j)),
        compiler_params=_cparams(("parallel", "parallel"), 40),
        name="mod",
    )(cond, w_mod, b_mod.reshape(DEPTH, 1, n))


def _prenorm_kernel(xp_ref, xs_ref, sh_ref, sc_ref, x_ref, h_ref, *, n_p):
    def emit(src_ref):
        x_ref[...] = src_ref[...]
        h_ref[...] = _norm_mod(src_ref[...], sh_ref[...], sc_ref[...]).astype(BF16)

    i = pl.program_id(0)
    pl.when(i < n_p)(lambda: emit(xp_ref))
    pl.when(i >= n_p)(lambda: emit(xs_ref))


def _prenorm_call(x_prompt, x_sample, mod5):
    tm = 512
    n_p = T_P // tm
    return pl.pallas_call(
        functools.partial(_prenorm_kernel, n_p=n_p),
        out_shape=(jax.ShapeDtypeStruct((T, D_MODEL), F32), jax.ShapeDtypeStruct((T, D_MODEL), BF16)),
        grid=(T // tm,),
        in_specs=[
            pl.BlockSpec((tm, D_MODEL), lambda i: (jnp.minimum(i, n_p - 1), 0)),
            pl.BlockSpec((tm, D_MODEL), lambda i: (jnp.maximum(i - n_p, 0), 0)),
            _mod_spec(0, 0, tm),
            _mod_spec(0, 1, tm),
        ],
        out_specs=(pl.BlockSpec((tm, D_MODEL), lambda i: (i, 0)),
                   pl.BlockSpec((tm, D_MODEL), lambda i: (i, 0))),
        compiler_params=_cparams(("arbitrary",), 40),
        name="prenorm",
    )(x_prompt.reshape(T_P, D_MODEL), x_sample.reshape(T_S, D_MODEL), mod5, mod5)


def _ffn_up_kernel(h_ref, wg_ref, wu_ref, a_ref):
    wg = wg_ref[...].astype(BF16)
    wu = wu_ref[...].astype(BF16)
    sub = h_ref.shape[0] // 2
    for s in range(2):
        rows = slice(s * sub, (s + 1) * sub)
        h = h_ref[rows, :]
        a_ref[rows, :] = (jax.nn.silu(_dot(h, wg)) * _dot(h, wu)).astype(BF16)


def _finish_norm(xo_ref, ho_ref, ssq, sh_ref, sc_ref, chunks):
    r = lax.rsqrt(ssq / D_MODEL + EPS)
    for cols in chunks:
        y = xo_ref[:, cols] * r * (1 + sc_ref[:, cols]) + sh_ref[:, cols]
        ho_ref[:, cols] = y.astype(BF16)


def _ffn_down_kernel(a_ref, at_ref, wd_ref, wdt_ref, xs_ref, g_ref, *rest, nk, emit_h):
    if emit_h:
        sh_ref, sc_ref, xo_ref, ho_ref = rest
    else:
        (xo_ref,) = rest
    k = pl.program_id(1)
    tn = D_MODEL // nk
    half_g = 0.5 * g_ref[...]
    chunks = [slice(n * tn, (n + 1) * tn) for n in range(nk)]

    def step_product(cols, with_tail=False):
        d = _dot(a_ref[...], wd_ref[:, cols])
        if with_tail:
            d = d + _dot(at_ref[...], wdt_ref[:, cols])
        return half_g[:, cols] * d

    @pl.when(k == 0)
    def _():
        xo_ref[...] = jnp.zeros_like(xo_ref)

    @pl.when(k < nk - 1)
    def _():
        for cols in chunks:
            xo_ref[:, cols] += step_product(cols)
        for n in range(nk - 1):
            @pl.when(k == n)
            def _(n=n):
                xo_ref[:, chunks[n]] += xs_ref[...]

    @pl.when(k == nk - 1)
    def _():
        ssq = jnp.zeros((xo_ref.shape[0], 1), F32)
        for n, cols in enumerate(chunks):
            val = xo_ref[:, cols] + step_product(cols, with_tail=True)
            if n == nk - 1:
                val = val + xs_ref[...]
            xo_ref[:, cols] = val
            if emit_h:
                ssq = ssq + jnp.sum(val * val, axis=-1, keepdims=True)
        if emit_h:
            _finish_norm(xo_ref, ho_ref, ssq, sh_ref, sc_ref, chunks)


def _ffn_up_call(h, w_gu, layer):
    tm, tf = 2048, 512
    nf = D_FF // tf
    return pl.pallas_call(
        _ffn_up_kernel,
        out_shape=jax.ShapeDtypeStruct((T, D_FF), BF16),
        grid=(T // tm, nf),
        in_specs=[
            pl.BlockSpec((tm, D_MODEL), lambda i, j: (i, 0)),
            pl.BlockSpec((None, D_MODEL, tf), lambda i, j: (layer, 0, j)),
            pl.BlockSpec((None, D_MODEL, tf), lambda i, j: (layer, 0, j + nf)),
        ],
        out_specs=pl.BlockSpec((tm, tf), lambda i, j: (i, j)),
        compiler_params=_cparams(("parallel", "parallel"), 52),
        name="ffn_up",
    )(h, w_gu, w_gu)


def _ffn_down_call(a, w_d, x, mod5, layer, gate_chunk, next_mod=None, rows=(0, T)):
    tm, nk = 1024, 4
    tk = (D_FF // MXU_DEPTH // nk) * MXU_DEPTH
    tail = D_FF - nk * tk
    assert tail > 0 and (nk * tk) % tail == 0
    tail_blk = nk * tk // tail
    t0, n_rows = rows[0] // tm, rows[1] - rows[0]
    emit_h = next_mod is not None
    in_specs = [
        pl.BlockSpec((tm, tk), lambda i, k: (t0 + i, k)),
        pl.BlockSpec((tm, tail), lambda i, k: (t0 + i, tail_blk)),
        pl.BlockSpec((None, tk, D_MODEL), lambda i, k: (layer, k, 0)),
        pl.BlockSpec((None, tail, D_MODEL), lambda i, k: (layer, tail_blk, 0)),
        pl.BlockSpec((tm, D_MODEL // nk), lambda i, k: (t0 + i, k)),
        _mod_spec(layer, gate_chunk, tm, t0),
    ]
    args = [a, a, w_d, w_d, x, mod5]
    out_shape = [jax.ShapeDtypeStruct((n_rows, D_MODEL), F32)]
    out_specs = [pl.BlockSpec((tm, D_MODEL), lambda i, k: (i, 0))]
    if emit_h:
        in_specs += [_mod_spec(next_mod[0], next_mod[1], tm, t0), _mod_spec(next_mod[0], next_mod[1] + 1, tm, t0)]
        args += [mod5, mod5]
        out_shape.append(jax.ShapeDtypeStruct((n_rows, D_MODEL), BF16))
        out_specs.append(pl.BlockSpec((tm, D_MODEL), lambda i, k: (i, 0)))
    return pl.pallas_call(
        functools.partial(_ffn_down_kernel, nk=nk, emit_h=emit_h),
        out_shape=tuple(out_shape),
        grid=(n_rows // tm, nk),
        in_specs=in_specs,
        out_specs=tuple(out_specs),
        compiler_params=_cparams(("parallel", "arbitrary"), 58),
        name="ffn_down",
    )(*args)


def _cm_kernel(h_ref, w_ref, nw_ref, wsp_ref, bsp_ref, y_ref):
    h = h_ref[...]
    v = jax.nn.gelu(_dot(h, w_ref[:, CM_WIDTH:]))
    vn = (_rms(v) * nw_ref[...]).astype(BF16)
    u = jax.nn.gelu(_dot(h, w_ref[:, :CM_WIDTH]))
    for c in range(h_ref.shape[0] // CHUNK):
        rows = slice(c * CHUNK, (c + 1) * CHUNK)
        for g in range(CM_GROUPS):
            cols = slice(g * CHUNK, (g + 1) * CHUNK)
            mixed = _dot(wsp_ref[g], vn[rows, cols]) + bsp_ref[g]
            y_ref[rows, cols] = (u[rows, cols] * mixed).astype(BF16)


def _cm_call(h, w_cm, cm_nw, wsp, bsp, layer):
    tm = 512
    return pl.pallas_call(
        _cm_kernel,
        out_shape=jax.ShapeDtypeStruct((T, CM_WIDTH), BF16),
        grid=(T // tm,),
        in_specs=[
            pl.BlockSpec((tm, D_MODEL), lambda i: (i, 0)),
            pl.BlockSpec((None, D_MODEL, 2 * CM_WIDTH), lambda i: (layer, 0, 0)),
            pl.BlockSpec((None, 1, CM_WIDTH), lambda i: (layer, 0, 0)),
            pl.BlockSpec((None, CM_GROUPS, CHUNK, CHUNK), lambda i: (layer, 0, 0, 0)),
            pl.BlockSpec((None, CM_GROUPS, CHUNK, CHUNK), lambda i: (layer, 0, 0, 0)),
        ],
        out_specs=pl.BlockSpec((tm, CM_WIDTH), lambda i: (i, 0)),
        compiler_params=_cparams(("parallel",), 48),
        name="chunk_mlp",
    )(h, w_cm, cm_nw, wsp, bsp)


def _retproj_kernel(h_ref, w_ref, s_ref, o_ref):
    o_ref[...] = (_dot(h_ref[...], w_ref[...]) * s_ref[...]).astype(BF16)


def _retproj_call(h, w_ret, col_scale, layer):
    tm, tn = 1024, 1024
    n = 2 * RET_QK_WIDTH + 2 * RET_V_WIDTH
    return pl.pallas_call(
        _retproj_kernel,
        out_shape=jax.ShapeDtypeStruct((T, n), BF16),
        grid=(T // tm, n // tn),
        in_specs=[
            pl.BlockSpec((tm, D_MODEL), lambda i, j: (i, 0)),
            pl.BlockSpec((None, D_MODEL, tn), lambda i, j: (layer, 0, W_RET_COL // tn + j)),
            pl.BlockSpec((1, tn), lambda i, j: (0, j)),
        ],
        out_specs=pl.BlockSpec((tm, tn), lambda i, j: (i, j)),
        compiler_params=_cparams(("parallel", "parallel"), 40),
        name="ret_proj",
    )(h, w_ret, col_scale)


def _ret_kernel(*refs, layer, n_chunks, has_s0, emit_state):
    refs = list(refs)
    lg_ref, q_ref, k_ref, v_ref, g_ref, nw_ref = refs[:6]
    pos = 6
    s0_ref = None
    if has_s0:
        s0_ref = refs[pos]
        pos += 1
    pos += 1
    y_ref = refs[pos]
    pos += 1
    so_ref = None
    if emit_state:
        so_ref = refs[pos]
        pos += 1
    yacc_refs = refs[pos:pos + 2]

    head = pl.program_id(1)
    row = lax.broadcasted_iota(jnp.int32, (CHUNK, CHUNK), 0).astype(F32)
    col = lax.broadcasted_iota(jnp.int32, (CHUNK, CHUNK), 1).astype(F32)
    tables = []
    for d in range(2):
        lg = lg_ref[layer * 2 * RET_HEADS + d * RET_HEADS + head]
        if d == 0:
            rel = row - col
            q_decay = jnp.exp(lg * (row + 1.0))
            k_decay = jnp.exp(lg * (CHUNK - 1.0 - row))
        else:
            rel = col - row
            q_decay = jnp.exp(lg * (CHUNK - row))
            k_decay = jnp.exp(lg * row)
        intra = jnp.where(rel >= 0, jnp.exp(lg * jnp.maximum(rel, 0.0)), 0.0)
        chunk_decay = jnp.exp(jnp.full((RET_DK, RET_DV), lg * CHUNK, F32))
        tables.append((intra, q_decay, k_decay, chunk_decay))

    def body(t, states):
        new_states = []
        for d in range(2):
            intra, q_decay, k_decay, chunk_decay = tables[d]
            c = t if d == 0 else n_chunks - 1 - t
            rows = pl.ds(pl.multiple_of(c * CHUNK, CHUNK), CHUNK)
            qc = q_ref[rows, :]
            kc = k_ref[rows, :]
            vc = v_ref[rows, :]
            scores = (_dot_nt(qc, kc) * intra).astype(BF16)
            qd = (qc.astype(F32) * q_decay).astype(BF16)
            kd = (kc.astype(F32) * k_decay).astype(BF16)
            yacc_refs[d][rows, :] = _dot(scores, vc) + _dot(qd, states[d].astype(BF16))
            new_states.append(chunk_decay * states[d] + _dot_tn(kd, vc))
        return tuple(new_states)

    if has_s0:
        s_init = (s0_ref[0], s0_ref[1])
    else:
        s_init = (jnp.zeros((RET_DK, RET_DV), F32), jnp.zeros((RET_DK, RET_DV), F32))
    s_final = lax.fori_loop(0, n_chunks, body, s_init, unroll=min(4, n_chunks))
    if emit_state:
        so_ref[0] = s_final[0]
        so_ref[1] = s_final[1]

    y = yacc_refs[0][...] + yacc_refs[1][...]
    mu = jnp.mean(y, axis=-1, keepdims=True)
    var = jnp.mean(jnp.square(y - mu), axis=-1, keepdims=True)
    yn = (y - mu) * lax.rsqrt(var + EPS) * nw_ref[...]
    y_ref[...] = (jax.nn.silu(g_ref[...].astype(F32)) * yn).astype(BF16)


def _ret_call(lg, rp, ret_nw, s0, y_prev, layer, *, prompt):
    seq = SEQ if prompt else DEC_SEQ
    nb = BATCH if prompt else DEC_BATCH
    rb0 = 0 if prompt else T_P // DEC_SEQ
    kq = RET_QK_WIDTH // RET_DK
    kv = 2 * RET_QK_WIDTH // RET_DV
    kg = kv + RET_HEADS
    in_specs = [
        pl.BlockSpec(memory_space=pltpu.SMEM),
        pl.BlockSpec((seq, RET_DK), lambda b, h: (rb0 + b, h)),
        pl.BlockSpec((seq, RET_DK), lambda b, h: (rb0 + b, kq + h)),
        pl.BlockSpec((seq, RET_DV), lambda b, h: (rb0 + b, kv + h)),
        pl.BlockSpec((seq, RET_DV), lambda b, h: (rb0 + b, kg + h)),
        pl.BlockSpec((None, 1, RET_DV), lambda b, h: (layer, 0, h)),
    ]
    args = [lg, rp, rp, rp, rp, ret_nw]
    if not prompt:
        in_specs.append(pl.BlockSpec((None, None, 2, None, RET_DK, RET_DV), lambda b, h: (b, layer, 0, h, 0, 0)))
        args.append(s0)
    in_specs.append(pl.BlockSpec(memory_space=pl.ANY))
    args.append(y_prev)
    out_shape = [jax.ShapeDtypeStruct((T, RET_V_WIDTH), BF16)]
    out_specs = [pl.BlockSpec((seq, RET_DV), lambda b, h: (rb0 + b, h))]
    if prompt:
        out_shape.append(jax.ShapeDtypeStruct((BATCH, 2, RET_HEADS, RET_DK, RET_DV), F32))
        out_specs.append(pl.BlockSpec((None, 2, None, RET_DK, RET_DV), lambda b, h: (b, 0, h, 0, 0)))
    return pl.pallas_call(
        functools.partial(_ret_kernel, layer=layer, n_chunks=seq // CHUNK, has_s0=not prompt,
                          emit_state=prompt),
        out_shape=tuple(out_shape),
        grid=(nb, RET_HEADS),
        in_specs=in_specs,
        out_specs=tuple(out_specs),
        scratch_shapes=[pltpu.VMEM((seq, RET_DV), F32), pltpu.VMEM((seq, RET_DV), F32)],
        input_output_aliases={len(args) - 1: 0},
        compiler_params=_cparams(("parallel", "parallel"), 32),
        name="retention_prompt" if prompt else "retention_sample",
    )(*args)


def _rope_lanes(x, cos_t, sin_t):
    return x * cos_t + pltpu.roll(x, 64, 1) * sin_t


def _head_keys(kvf, kr2, rotk, wkn, k_ref, v_ref, rows):
    lane = lax.broadcasted_iota(jnp.int32, (1, 2 * MLA_ROPE), 1)
    ssq_kr = jnp.sum(jnp.where(lane < MLA_ROPE, kr2 * kr2, 0.0), axis=-1, keepdims=True)
    nk = MLA_HEADS * MLA_NOPE
    for hh in range(MLA_HEADS):
        kn = kvf[:, hh * MLA_NOPE:(hh + 1) * MLA_NOPE]
        ssq = jnp.sum(kn * kn, axis=-1, keepdims=True) + ssq_kr
        r = lax.rsqrt(ssq / MLA_QK_DIM + EPS)
        k_ref[hh, rows, 0:MLA_NOPE] = (kn * r * wkn).astype(BF16)
        k_ref[hh, rows, MLA_NOPE:MLA_SLAB] = (rotk * r).astype(BF16)
        v_ref[hh, rows, :] = kvf[:, nk + hh * MLA_V: nk + (hh + 1) * MLA_V].astype(BF16)


MLA_ROWS = 256


def _mla_kernel(h_ref, w1_ref, wuq_ref, wukv_ref, qnw_ref, kvnw_ref, wq_ref, wkn_ref, wkr_ref, cos_ref, sin_ref,
                q_ref, k_ref, v_ref, ckv_ref, kr_ref):
    lane = lax.broadcasted_iota(jnp.int32, (1, MLA_SLAB), 1)
    for c in range(h_ref.shape[0] // MLA_ROWS):
        rows = slice(c * MLA_ROWS, (c + 1) * MLA_ROWS)
        p = _dot(h_ref[rows, :], w1_ref[...])
        cq = _rms(p[:, :MLA_Q_RANK]) * qnw_ref[...]
        ckv = _rms(p[:, MLA_Q_RANK:MLA_Q_RANK + MLA_KV_RANK]) * kvnw_ref[...]
        kr2 = p[:, MLA_Q_RANK + MLA_KV_RANK:]
        ckv_ref[rows, :] = ckv
        kr_ref[rows, :] = kr2[:, :MLA_ROPE]
        cos_t = cos_ref[rows, :]
        sin_t = sin_ref[rows, :]
        qf = _dot(cq.astype(BF16), wuq_ref[...])
        kvf = _dot(ckv.astype(BF16), wukv_ref[...])
        for hh in range(MLA_HEADS):
            slab = qf[:, hh * MLA_SLAB:(hh + 1) * MLA_SLAB]
            ssq = jnp.sum(jnp.where(lane < MLA_QK_DIM, slab * slab, 0.0), axis=-1, keepdims=True)
            n = slab * lax.rsqrt(ssq / MLA_QK_DIM + EPS) * wq_ref[...]
            q_ref[hh, rows, 0:MLA_NOPE] = n[:, :MLA_NOPE].astype(BF16)
            q_ref[hh, rows, MLA_NOPE:MLA_SLAB] = _rope_lanes(n[:, MLA_NOPE:], cos_t, sin_t).astype(BF16)
        rotk = _rope_lanes(kr2 * wkr_ref[...], cos_t, sin_t)
        _head_keys(kvf, kr2, rotk, wkn_ref[...], k_ref, v_ref, rows)


ROPE_TILE = 512


def _mla_call(h, w1, wuq, wukv, qnw, kvnw, wq, wkn, wkr, cos_t, sin_t, layer):
    tm = ROPE_TILE
    n1 = MLA_Q_RANK + MLA_KV_RANK + 2 * MLA_ROPE
    lsel = lambda i: (layer, 0, 0)
    n_p = T_P // tm

    def rope_blk(i):
        return (jnp.where(i < n_p, 0, 1 + (i - n_p) % (DEC_SEQ // tm)), 0)

    return pl.pallas_call(
        _mla_kernel,
        out_shape=(
            jax.ShapeDtypeStruct((MLA_HEADS, T, MLA_SLAB), BF16),
            jax.ShapeDtypeStruct((MLA_HEADS, T, MLA_SLAB), BF16),
            jax.ShapeDtypeStruct((MLA_HEADS, T, MLA_V), BF16),
            jax.ShapeDtypeStruct((T, MLA_KV_RANK), F32),
            jax.ShapeDtypeStruct((T, MLA_ROPE), F32),
        ),
        grid=(T // tm,),
        in_specs=[
            pl.BlockSpec((tm, D_MODEL), lambda i: (i, 0)),
            pl.BlockSpec((None, D_MODEL, n1), lsel),
            pl.BlockSpec((None, MLA_Q_RANK, MLA_HEADS * MLA_SLAB), lsel),
            pl.BlockSpec((None, MLA_KV_RANK, MLA_HEADS * (MLA_NOPE + MLA_V)), lsel),
            pl.BlockSpec((None, 1, MLA_Q_RANK), lsel),
            pl.BlockSpec((None, 1, MLA_KV_RANK), lsel),
            pl.BlockSpec((None, 1, MLA_SLAB), lsel),
            pl.BlockSpec((None, 1, MLA_NOPE), lsel),
            pl.BlockSpec((None, 1, 2 * MLA_ROPE), lsel),
            pl.BlockSpec((tm, 2 * MLA_ROPE), rope_blk),
            pl.BlockSpec((tm, 2 * MLA_ROPE), rope_blk),
        ],
        out_specs=(
            pl.BlockSpec((MLA_HEADS, tm, MLA_SLAB), lambda i: (0, i, 0)),
            pl.BlockSpec((MLA_HEADS, tm, MLA_SLAB), lambda i: (0, i, 0)),
            pl.BlockSpec((MLA_HEADS, tm, MLA_V), lambda i: (0, i, 0)),
            pl.BlockSpec((tm, MLA_KV_RANK), lambda i: (i, 0)),
            pl.BlockSpec((tm, MLA_ROPE), lambda i: (i, 0)),
        ),
        compiler_params=_cparams(("parallel",), 48),
        name="mla_proj",
    )(h, w1, wuq, wukv, qnw, kvnw, wq, wkn, wkr, cos_t, sin_t)


def _ctxkv_kernel(c_ref, kr_ref, wukv_ref, wkn_ref, wkr_ref, k_ref, v_ref):
    kvf = _dot(c_ref[...].astype(BF16), wukv_ref[...])
    kr2 = kr_ref[...]
    _head_keys(kvf, kr2, kr2 * wkr_ref[...], wkn_ref[...], k_ref, v_ref, slice(None))


def _ctxkv_call(cache_ckv, cache_kr_pad, wukv, wkn, wkr):
    lsel = lambda l, b: (l, 0, 0)
    return pl.pallas_call(
        _ctxkv_kernel,
        out_shape=(
            jax.ShapeDtypeStruct((DEPTH, MLA_HEADS, DEC_BATCH * PAST_LEN, MLA_SLAB), BF16),
            jax.ShapeDtypeStruct((DEPTH, MLA_HEADS, DEC_BATCH * PAST_LEN, MLA_V), BF16),
        ),
        grid=(DEPTH, DEC_BATCH),
        in_specs=[
            pl.BlockSpec((None, None, PAST_LEN, MLA_KV_RANK), lambda l, b: (b, l, 0, 0)),
            pl.BlockSpec((None, None, PAST_LEN, 2 * MLA_ROPE), lambda l, b: (b, l, 0, 0)),
            pl.BlockSpec((None, MLA_KV_RANK, MLA_HEADS * (MLA_NOPE + MLA_V)), lsel),
            pl.BlockSpec((None, 1, MLA_NOPE), lsel),
            pl.BlockSpec((None, 1, 2 * MLA_ROPE), lsel),
        ],
        out_specs=(
            pl.BlockSpec((None, MLA_HEADS, PAST_LEN, MLA_SLAB), lambda l, b: (l, 0, b, 0)),
            pl.BlockSpec((None, MLA_HEADS, PAST_LEN, MLA_V), lambda l, b: (l, 0, b, 0)),
        ),
        compiler_params=_cparams(("parallel", "parallel"), 32),
        name="ctx_kv",
    )(cache_ckv, cache_kr_pad, wukv, wkn, wkr)


def _attend(q, keys, values):
    scores = [_dot_nt(q, k) for k in keys]
    m = functools.reduce(jnp.maximum, [jnp.max(s, axis=-1, keepdims=True) for s in scores])
    probs = [jnp.exp(s - m) for s in scores]
    denom = sum(jnp.sum(p, axis=-1, keepdims=True) for p in probs)
    o = sum(_dot(p.astype(BF16), v) for p, v in zip(probs, values))
    return (o / denom).astype(BF16)


ATTN_ROWS = 256


def _attn_prompt_kernel(q_ref, k_ref, v_ref, _, o_ref):
    for hh in range(MLA_HEADS):
        o_ref[:, hh * MLA_V:(hh + 1) * MLA_V] = _attend(q_ref[hh], [k_ref[hh]], [v_ref[hh]])


def _attn_sample_kernel(q_ref, ks_ref, vs_ref, kc_ref, vc_ref, _, o_ref):
    for r in range(q_ref.shape[0] // ATTN_ROWS):
        rows = slice(r * ATTN_ROWS, (r + 1) * ATTN_ROWS)
        o_ref[rows, :] = _attend(q_ref[rows, :], [ks_ref[...], kc_ref[...]], [vs_ref[...], vc_ref[...]])


def _attn_prompt_call(q, k, v, y_prev):
    return pl.pallas_call(
        _attn_prompt_kernel,
        out_shape=jax.ShapeDtypeStruct((T, MLA_HEADS * MLA_V), BF16),
        grid=(BATCH,),
        in_specs=[
            pl.BlockSpec((MLA_HEADS, SEQ, MLA_SLAB), lambda b: (0, b, 0)),
            pl.BlockSpec((MLA_HEADS, SEQ, MLA_SLAB), lambda b: (0, b, 0)),
            pl.BlockSpec((MLA_HEADS, SEQ, MLA_V), lambda b: (0, b, 0)),
            pl.BlockSpec(memory_space=pl.ANY),
        ],
        out_specs=pl.BlockSpec((SEQ, MLA_HEADS * MLA_V), lambda b: (b, 0)),
        input_output_aliases={3: 0},
        compiler_params=_cparams(("parallel",), 32),
        name="attn_prompt",
    )(q, k, v, y_prev)


def _attn_sample_call(q, k, v, kc, vc, y_prev, layer):
    tq = 2048
    nq = DEC_SEQ // tq
    q0 = T_P // tq
    s0 = T_P // DEC_SEQ
    return pl.pallas_call(
        _attn_sample_kernel,
        out_shape=jax.ShapeDtypeStruct((T, MLA_HEADS * MLA_V), BF16),
        grid=(DEC_BATCH, MLA_HEADS, nq),
        in_specs=[
            pl.BlockSpec((None, tq, MLA_SLAB), lambda b, h, i: (h, q0 + b * nq + i, 0)),
            pl.BlockSpec((None, DEC_SEQ, MLA_SLAB), lambda b, h, i: (h, s0 + b, 0)),
            pl.BlockSpec((None, DEC_SEQ, MLA_V), lambda b, h, i: (h, s0 + b, 0)),
            pl.BlockSpec((None, None, PAST_LEN, MLA_SLAB), lambda b, h, i: (layer, h, b, 0)),
            pl.BlockSpec((None, None, PAST_LEN, MLA_V), lambda b, h, i: (layer, h, b, 0)),
            pl.BlockSpec(memory_space=pl.ANY),
        ],
        out_specs=pl.BlockSpec((tq, MLA_V), lambda b, h, i: (q0 + b * nq + i, h)),
        input_output_aliases={5: 0},
        compiler_params=_cparams(("parallel", "parallel", "parallel"), 48),
        name="attn_sample",
    )(q, k, v, kc, vc, y_prev)


def _merge_kernel(h_ref, ya_ref, yb_ref, yc_ref, wga_ref, wgb_ref, wgc_ref, wb_ref, o_ref):
    h = h_ref[...]
    m = jax.nn.sigmoid(_dot(h, wga_ref[...])) * _dot(ya_ref[...], wb_ref[0])
    m = m + jax.nn.sigmoid(_dot(h, wgb_ref[...])) * _dot(yb_ref[...], wb_ref[1])
    m = m + jax.nn.sigmoid(_dot(h, wgc_ref[...])) * _dot(yc_ref[...], wb_ref[2])
    o_ref[...] = m.astype(BF16)


def _merge_call(h, ya, yb, yc, w_g, w_b, layer):
    tm, tn = 1024, 512
    nj = D_MODEL // tn
    yspec = pl.BlockSpec((tm, BRANCH_WIDTH), lambda i, j: (i, 0))
    return pl.pallas_call(
        _merge_kernel,
        out_shape=jax.ShapeDtypeStruct((T, D_MODEL), BF16),
        grid=(T // tm, nj),
        in_specs=[
            pl.BlockSpec((tm, D_MODEL), lambda i, j: (i, 0)),
            yspec, yspec, yspec,
            pl.BlockSpec((None, D_MODEL, tn), lambda i, j: (layer, 0, j)),
            pl.BlockSpec((None, D_MODEL, tn), lambda i, j: (layer, 0, j + nj)),
            pl.BlockSpec((None, D_MODEL, tn), lambda i, j: (layer, 0, j + 2 * nj)),
            pl.BlockSpec((None, 3, BRANCH_WIDTH, tn), lambda i, j: (layer, 0, 0, j)),
        ],
        out_specs=pl.BlockSpec((tm, tn), lambda i, j: (i, j)),
        compiler_params=_cparams(("parallel", "parallel"), 56),
        name="gated_merge",
    )(h, ya, yb, yc, w_g, w_g, w_g, w_b)


def _outproj_kernel(m_ref, w_ref, x_ref, g_ref, sh_ref, sc_ref, xo_ref, ho_ref):
    xn = x_ref[...] + g_ref[...] * _dot(m_ref[...], w_ref[...])
    xo_ref[...] = xn
    ho_ref[...] = _norm_mod(xn, sh_ref[...], sc_ref[...]).astype(BF16)


def _outproj_call(m, w_out, x, mod5, layer):
    tm = 512
    return pl.pallas_call(
        _outproj_kernel,
        out_shape=(jax.ShapeDtypeStruct((T, D_MODEL), F32), jax.ShapeDtypeStruct((T, D_MODEL), BF16)),
        grid=(T // tm,),
        in_specs=[
            pl.BlockSpec((tm, D_MODEL), lambda i: (i, 0)),
            pl.BlockSpec((None, D_MODEL, D_MODEL), lambda i: (layer, 0, 0)),
            pl.BlockSpec((tm, D_MODEL), lambda i: (i, 0)),
            _mod_spec(layer, 5, tm),
            _mod_spec(layer, 6, tm),
            _mod_spec(layer, 7, tm),
        ],
        out_specs=(pl.BlockSpec((tm, D_MODEL), lambda i: (i, 0)),
                   pl.BlockSpec((tm, D_MODEL), lambda i: (i, 0))),
        compiler_params=_cparams(("parallel",), 52),
        name="out_proj",
    )(m, w_out, x, mod5, mod5, mod5)


def _swap_halves(x):
    half = x.shape[-1] // 2
    return jnp.concatenate([x[..., half:], x[..., :half]], axis=-1)


def _rope_tables():
    t = jnp.arange(DEC_SEQ)
    row = (t // GRID_W).astype(F32)
    col = (t % GRID_W).astype(F32)
    n_freq = MLA_ROPE // 4
    freqs = jnp.power(ROPE_THETA, -jnp.arange(n_freq, dtype=F32) / n_freq)
    ang = jnp.concatenate([row[:, None] * freqs, col[:, None] * freqs], axis=-1)
    cos, sin = jnp.cos(ang), jnp.sin(ang)
    zeros = jnp.zeros((DEC_SEQ, MLA_ROPE), F32)
    cos_s = jnp.concatenate([cos, cos, zeros], axis=-1)
    sin_s = jnp.concatenate([-sin, sin, zeros], axis=-1)
    cos_p = jnp.concatenate([jnp.ones((ROPE_TILE, MLA_ROPE), F32), jnp.zeros((ROPE_TILE, MLA_ROPE), F32)], axis=-1)
    sin_p = jnp.zeros((ROPE_TILE, 2 * MLA_ROPE), F32)
    return jnp.concatenate([cos_p, cos_s], axis=0), jnp.concatenate([sin_p, sin_s], axis=0)


def kernel(x_prompt, x_sample, cache_mla_ckv, cache_mla_krope, state_retention, c, c_ctx, w_mod, b_mod,
           ffn1_w_gate_up, ffn1_w_down, ffn2_w_gate_up, ffn2_w_down, w_in, cm_norm_w, cm_w_spatial,
           cm_b_spatial, ret_decay_logit, ret_norm_w, mla_q_norm_w, mla_kv_norm_w, mla_w_uq, mla_w_ukv,
           qk_norm_q_w, qk_norm_k_w, w_branch, w_out):
    w_main = w_in.astype(BF16)
    w_gate = w_main[:, :, W_GATE_COL:]
    w_kr = w_main[:, :, W_KR_COL:W_GATE_COL]
    w_mla1 = jnp.concatenate([w_main[:, :, W_MLA_COL:W_KR_COL], w_kr, _swap_halves(w_kr)], axis=-1)
    ffn1_d = ffn1_w_down.astype(BF16)
    ffn2_d = ffn2_w_down.astype(BF16)
    w_b = w_branch.astype(BF16)
    w_o = w_out.astype(BF16)

    uq = mla_w_uq.reshape(DEPTH, MLA_Q_RANK, MLA_HEADS, MLA_QK_DIM)
    uq_rope = uq[..., MLA_NOPE:]
    wuq = jnp.concatenate([uq[..., :MLA_NOPE], uq_rope, _swap_halves(uq_rope)], axis=-1)
    wuq = wuq.reshape(DEPTH, MLA_Q_RANK, MLA_HEADS * MLA_SLAB).astype(BF16)
    ukv = mla_w_ukv.reshape(DEPTH, MLA_KV_RANK, MLA_HEADS, MLA_NOPE + MLA_V)
    wukv = jnp.concatenate([ukv[..., :MLA_NOPE].reshape(DEPTH, MLA_KV_RANK, MLA_HEADS * MLA_NOPE),
                            ukv[..., MLA_NOPE:].reshape(DEPTH, MLA_KV_RANK, MLA_HEADS * MLA_V)],
                           axis=-1).astype(BF16)
    q_rope_w = qk_norm_q_w[:, MLA_NOPE:]
    wq = (jnp.concatenate([qk_norm_q_w[:, :MLA_NOPE], q_rope_w, _swap_halves(q_rope_w)], axis=-1)
          * (MLA_QK_DIM ** -0.5)).reshape(DEPTH, 1, MLA_SLAB)
    k_rope_w = qk_norm_k_w[:, MLA_NOPE:]
    wkn = qk_norm_k_w[:, :MLA_NOPE].reshape(DEPTH, 1, MLA_NOPE)
    wkr = jnp.concatenate([k_rope_w, _swap_halves(k_rope_w)], axis=-1).reshape(DEPTH, 1, 2 * MLA_ROPE)
    qnw = mla_q_norm_w.reshape(DEPTH, 1, MLA_Q_RANK)
    kvnw = mla_kv_norm_w.reshape(DEPTH, 1, MLA_KV_RANK)
    cm_nw = cm_norm_w.reshape(DEPTH, 1, CM_WIDTH)
    wsp = cm_w_spatial.astype(BF16)
    bsp = jnp.broadcast_to(cm_b_spatial[..., None], (DEPTH, CM_GROUPS, CHUNK, CHUNK))
    ret_nw = ret_norm_w.reshape(DEPTH, 1, RET_V_WIDTH)
    lg = jax.nn.log_sigmoid(ret_decay_logit.astype(F32)).reshape(-1)
    ret_scale = jnp.concatenate([jnp.ones((RET_QK_WIDTH,), F32), jnp.full((RET_QK_WIDTH,), RET_DK ** -0.5, F32),
                                 jnp.ones((2 * RET_V_WIDTH,), F32)]).reshape(1, -1)
    cos_t, sin_t = _rope_tables()
    cache_kr_pad = jnp.pad(cache_mla_krope, ((0, 0), (0, 0), (0, 0), (0, MLA_ROPE)))

    cond = jnp.concatenate([c_ctx[None, :], c, jnp.zeros((MOD_ROWS - 1 - DEC_BATCH, D_MODEL), F32)], axis=0)

    mod5 = _mod_call(cond, w_mod, b_mod).reshape(DEPTH, MOD_ROWS, N_MOD, 1, D_MODEL)
    kc, vc = _ctxkv_call(cache_mla_ckv, cache_kr_pad, wukv, wkn, wkr)
    x, h = _prenorm_call(x_prompt, x_sample, mod5)

    yb = jnp.zeros((T, RET_V_WIDTH), BF16)
    yc = jnp.zeros((T, MLA_HEADS * MLA_V), BF16)
    ckv_layers, krope_layers, ret_layers = [], [], []
    for l in range(DEPTH):
        x, h = _ffn_down_call(_ffn_up_call(h, ffn1_w_gate_up, l), ffn1_d, x, mod5, l, 2, next_mod=(l, 3))

        ya = _cm_call(h, w_main, cm_nw, wsp, bsp, l)

        rp = _retproj_call(h, w_main, ret_scale, l)
        yb, ret_state = _ret_call(lg, rp, ret_nw, None, yb, l, prompt=True)
        (yb,) = _ret_call(lg, rp, ret_nw, state_retention, yb, l, prompt=False)

        q, k, v, ckv, kr = _mla_call(h, w_mla1, wuq, wukv, qnw, kvnw, wq, wkn, wkr, cos_t, sin_t, l)
        yc = _attn_prompt_call(q, k, v, yc)
        yc = _attn_sample_call(q, k, v, kc, vc, yc, l)

        merged = _merge_call(h, ya, yb, yc, w_gate, w_b, l)
        x, h = _outproj_call(merged, w_o, x, mod5, l)

        a = _ffn_up_call(h, ffn2_w_gate_up, l)
        if l + 1 < DEPTH:
            x, h = _ffn_down_call(a, ffn2_d, x, mod5, l, 8, next_mod=(l + 1, 0))
        else:
            (y_prompt,) = _ffn_down_call(a, ffn2_d, x, mod5, l, 8, rows=(0, T_P))
            (y_sample,) = _ffn_down_call(a, ffn2_d, x, mod5, l, 8, rows=(T_P, T))

        ckv_layers.append(ckv[:T_P].reshape(BATCH, SEQ, MLA_KV_RANK))
        krope_layers.append(kr[:T_P].reshape(BATCH, SEQ, MLA_ROPE))
        ret_layers.append(ret_state)

    return (y_prompt.reshape(BATCH, SEQ, D_MODEL), y_sample.reshape(DEC_BATCH, DEC_SEQ, D_MODEL),
            jnp.stack(ckv_layers, axis=1), jnp.stack(krope_layers, axis=1), jnp.stack(ret_layers, axis=1))
```

```python
import functools

import jax
import jax.numpy as jnp
from jax import lax
from jax.experimental import pallas as pl
from jax.experimental.pallas import tpu as pltpu

F32 = jnp.float32
BF16 = jnp.bfloat16

D_MODEL = 2048
BATCH = 16
SEQ = 256
DEPTH = 4
DEC_BATCH = 8
DEC_SEQ = 2048
PAST_LEN = 512
GRID_W = 64
CHUNK = 128
D_FF = 5632
N_MOD = 9
EPS = 1e-6
ROPE_THETA = 10000.0
CM_GROUPS = 8
CM_WIDTH = 1024
RET_HEADS = 4
RET_DK = 128
RET_DV = 256
RET_QK_WIDTH = RET_HEADS * RET_DK
RET_V_WIDTH = RET_HEADS * RET_DV
MLA_HEADS = 8
MLA_NOPE = 128
MLA_ROPE = 64
MLA_V = 128
MLA_Q_RANK = 512
MLA_KV_RANK = 512
MLA_QK_DIM = MLA_NOPE + MLA_ROPE
MLA_SLAB = 256
BRANCH_WIDTH = 1024

W_RET_COL = 2 * CM_WIDTH
W_MLA_COL = W_RET_COL + 2 * RET_QK_WIDTH + 2 * RET_V_WIDTH
W_KR_COL = W_MLA_COL + MLA_Q_RANK + MLA_KV_RANK
W_GATE_COL = W_KR_COL + MLA_ROPE

T_P = BATCH * SEQ
T_S = DEC_BATCH * DEC_SEQ
T = T_P + T_S
MOD_ROWS = 16

LOG2_E = 1.4426950408889634
MIB = 1024 * 1024
MXU_DEPTH = 256


def _cparams(semantics, vmem_mib):
    return pltpu.CompilerParams(dimension_semantics=semantics, vmem_limit_bytes=vmem_mib * MIB)


def _mod_row(i, tm):
    n_p = T_P // tm
    return jnp.where(i < n_p, 0, 1 + (i - n_p) // (DEC_SEQ // tm))


def _mod_spec(layer, chunk, tm, tile0=0):
    return pl.BlockSpec((None, None, None, 1, D_MODEL),
                        lambda i, *_: (layer, _mod_row(i + tile0, tm), chunk, 0, 0))


def _rms(x):
    return x * lax.rsqrt(jnp.mean(x * x, axis=-1, keepdims=True) + EPS)


def _norm_mod(x, sh, sc):
    return _rms(x) * (1 + sc) + sh


def _dot(a, b):
    return jnp.dot(a, b, preferred_element_type=F32)


def _dot_nt(a, b):
    return lax.dot_general(a, b, (((1,), (1,)), ((), ())), preferred_element_type=F32)


def _dot_tn(a, b):
    return lax.dot_general(a, b, (((0,), (0,)), ((), ())), preferred_element_type=F32)


def _mod_kernel(c_ref, w_ref, b_ref, o_ref):
    s = jax.nn.silu(c_ref[...]).astype(BF16)
    o_ref[...] = _dot(s, w_ref[...].astype(BF16)) + b_ref[...]


def _mod_call(cond, w_mod, b_mod):
    tn = 1024
    n = N_MOD * D_MODEL
    return pl.pallas_call(
        _mod_kernel,
        out_shape=jax.ShapeDtypeStruct((DEPTH, MOD_ROWS, n), F32),
        grid=(DEPTH, n // tn),
        in_specs=[
            pl.BlockSpec((MOD_ROWS, D_MODEL), lambda l, j: (0, 0)),
            pl.BlockSpec((None, D_MODEL, tn), lambda l, j: (l, 0, j)),
            pl.BlockSpec((None, 1, tn), lambda l, j: (l, 0, j)),
        ],
        out_specs=pl.BlockSpec((None, MOD_ROWS, tn), lambda l, j: (l, 0, j)),
        compiler_params=_cparams(("parallel", "parallel"), 40),
        name="mod",
    )(cond, w_mod, b_mod.reshape(DEPTH, 1, n))


def _prenorm_kernel(xp_ref, xs_ref, sh_ref, sc_ref, x_ref, h_ref, *, n_p):
    def emit(src_ref):
        x_ref[...] = src_ref[...]
        h_ref[...] = _norm_mod(src_ref[...], sh_ref[...], sc_ref[...]).astype(BF16)

    i = pl.program_id(0)
    pl.when(i < n_p)(lambda: emit(xp_ref))
    pl.when(i >= n_p)(lambda: emit(xs_ref))


def _prenorm_call(x_prompt, x_sample, mod5):
    tm = 512
    n_p = T_P // tm
    return pl.pallas_call(
        functools.partial(_prenorm_kernel, n_p=n_p),
        out_shape=(jax.ShapeDtypeStruct((T, D_MODEL), F32), jax.ShapeDtypeStruct((T, D_MODEL), BF16)),
        grid=(T // tm,),
        in_specs=[
            pl.BlockSpec((tm, D_MODEL), lambda i: (jnp.minimum(i, n_p - 1), 0)),
            pl.BlockSpec((tm, D_MODEL), lambda i: (jnp.maximum(i - n_p, 0), 0)),
            _mod_spec(0, 0, tm),
            _mod_spec(0, 1, tm),
        ],
        out_specs=(pl.BlockSpec((tm, D_MODEL), lambda i: (i, 0)),
                   pl.BlockSpec((tm, D_MODEL), lambda i: (i, 0))),
        compiler_params=_cparams(("arbitrary",), 40),
        name="prenorm",
    )(x_prompt.reshape(T_P, D_MODEL), x_sample.reshape(T_S, D_MODEL), mod5, mod5)


def _ffn_up_kernel(h_ref, wg_ref, wu_ref, wd_ref, a_ref, wd16_ref):
    @pl.when(pl.program_id(0) == 0)
    def _():
        wd16_ref[...] = wd_ref[...].astype(BF16)

    wg = wg_ref[...].astype(BF16)
    wu = wu_ref[...].astype(BF16)
    sub = h_ref.shape[0] // 2
    for s in range(2):
        rows = slice(s * sub, (s + 1) * sub)
        h = h_ref[rows, :]
        a_ref[rows, :] = (jax.nn.silu(_dot(h, wg)) * _dot(h, wu)).astype(BF16)


def _finish_norm(xo_ref, ho_ref, ssq, sh_ref, sc_ref, chunks):
    r = lax.rsqrt(ssq / D_MODEL + EPS)
    for cols in chunks:
        y = xo_ref[:, cols] * r * (1 + sc_ref[:, cols]) + sh_ref[:, cols]
        ho_ref[:, cols] = y.astype(BF16)


def _ffn_down_kernel(a_ref, at_ref, wd_ref, wdt_ref, xs_ref, g_ref, *rest, nk, emit_h):
    if emit_h:
        sh_ref, sc_ref, xo_ref, ho_ref = rest
    else:
        (xo_ref,) = rest
    k = pl.program_id(1)
    tn = D_MODEL // nk
    half_g = 0.5 * g_ref[...]
    chunks = [slice(n * tn, (n + 1) * tn) for n in range(nk)]

    def step_product(cols, with_tail=False):
        d = _dot(a_ref[...], wd_ref[:, cols])
        if with_tail:
            d = d + _dot(at_ref[...], wdt_ref[:, cols])
        return half_g[:, cols] * d

    @pl.when(k == 0)
    def _():
        xo_ref[...] = jnp.zeros_like(xo_ref)

    @pl.when(k < nk - 1)
    def _():
        for cols in chunks:
            xo_ref[:, cols] += step_product(cols)
        for n in range(nk - 1):
            @pl.when(k == n)
            def _(n=n):
                xo_ref[:, chunks[n]] += xs_ref[...]

    @pl.when(k == nk - 1)
    def _():
        ssq = jnp.zeros((xo_ref.shape[0], 1), F32)
        for n, cols in enumerate(chunks):
            val = xo_ref[:, cols] + step_product(cols, with_tail=True)
            if n == nk - 1:
                val = val + xs_ref[...]
            xo_ref[:, cols] = val
            if emit_h:
                ssq = ssq + jnp.sum(val * val, axis=-1, keepdims=True)
        if emit_h:
            _finish_norm(xo_ref, ho_ref, ssq, sh_ref, sc_ref, chunks)


def _ffn_up_call(h, w_gu, w_d, layer):
    tm, tf = 2048, 512
    nf = D_FF // tf

    def wd_blk(i, j):
        return jnp.where(i == 0, j, nf - 1)

    return pl.pallas_call(
        _ffn_up_kernel,
        out_shape=(jax.ShapeDtypeStruct((T, D_FF), BF16), jax.ShapeDtypeStruct((D_FF, D_MODEL), BF16)),
        grid=(T // tm, nf),
        in_specs=[
            pl.BlockSpec((tm, D_MODEL), lambda i, j: (i, 0)),
            pl.BlockSpec((None, D_MODEL, tf), lambda i, j: (layer, 0, j)),
            pl.BlockSpec((None, D_MODEL, tf), lambda i, j: (layer, 0, j + nf)),
            pl.BlockSpec((None, tf, D_MODEL), lambda i, j: (layer, wd_blk(i, j), 0)),
        ],
        out_specs=(pl.BlockSpec((tm, tf), lambda i, j: (i, j)),
                   pl.BlockSpec((tf, D_MODEL), lambda i, j: (wd_blk(i, j), 0))),
        compiler_params=_cparams(("arbitrary", "arbitrary"), 58),
        name="ffn_up",
    )(h, w_gu, w_gu, w_d)


def _ffn_down_call(a, w_d, x, mod5, layer, gate_chunk, next_mod=None, rows=(0, T)):
    tm, nk = 1024, 4
    tk = (D_FF // MXU_DEPTH // nk) * MXU_DEPTH
    tail = D_FF - nk * tk
    assert tail > 0 and (nk * tk) % tail == 0
    tail_blk = nk * tk // tail
    t0, n_rows = rows[0] // tm, rows[1] - rows[0]
    emit_h = next_mod is not None
    in_specs = [
        pl.BlockSpec((tm, tk), lambda i, k: (t0 + i, k)),
        pl.BlockSpec((tm, tail), lambda i, k: (t0 + i, tail_blk)),
        pl.BlockSpec((tk, D_MODEL), lambda i, k: (k, 0)),
        pl.BlockSpec((tail, D_MODEL), lambda i, k: (tail_blk, 0)),
        pl.BlockSpec((tm, D_MODEL // nk), lambda i, k: (t0 + i, k)),
        _mod_spec(layer, gate_chunk, tm, t0),
    ]
    args = [a, a, w_d, w_d, x, mod5]
    out_shape = [jax.ShapeDtypeStruct((n_rows, D_MODEL), F32)]
    out_specs = [pl.BlockSpec((tm, D_MODEL), lambda i, k: (i, 0))]
    if emit_h:
        in_specs += [_mod_spec(next_mod[0], next_mod[1], tm, t0), _mod_spec(next_mod[0], next_mod[1] + 1, tm, t0)]
        args += [mod5, mod5]
        out_shape.append(jax.ShapeDtypeStruct((n_rows, D_MODEL), BF16))
        out_specs.append(pl.BlockSpec((tm, D_MODEL), lambda i, k: (i, 0)))
    return pl.pallas_call(
        functools.partial(_ffn_down_kernel, nk=nk, emit_h=emit_h),
        out_shape=tuple(out_shape),
        grid=(n_rows // tm, nk),
        in_specs=in_specs,
        out_specs=tuple(out_specs),
        compiler_params=_cparams(("parallel", "arbitrary"), 58),
        name="ffn_down",
    )(*args)


def _cm_kernel(h_ref, w_ref, nw_ref, wsp_ref, bsp_ref, y_ref):
    h = h_ref[...]
    v = jax.nn.gelu(_dot(h, w_ref[:, CM_WIDTH:]))
    vn = (_rms(v) * nw_ref[...]).astype(BF16)
    u = jax.nn.gelu(_dot(h, w_ref[:, :CM_WIDTH]))
    for c in range(h_ref.shape[0] // CHUNK):
        rows = slice(c * CHUNK, (c + 1) * CHUNK)
        for g in range(CM_GROUPS):
            cols = slice(g * CHUNK, (g + 1) * CHUNK)
            mixed = _dot(wsp_ref[g], vn[rows, cols]) + bsp_ref[g]
            y_ref[rows, cols] = (u[rows, cols] * mixed).astype(BF16)


def _cm_call(h, w_cm, cm_nw, wsp, bsp, layer):
    tm = 512
    return pl.pallas_call(
        _cm_kernel,
        out_shape=jax.ShapeDtypeStruct((T, CM_WIDTH), BF16),
        grid=(T // tm,),
        in_specs=[
            pl.BlockSpec((tm, D_MODEL), lambda i: (i, 0)),
            pl.BlockSpec((None, D_MODEL, 2 * CM_WIDTH), lambda i: (layer, 0, 0)),
            pl.BlockSpec((None, 1, CM_WIDTH), lambda i: (layer, 0, 0)),
            pl.BlockSpec((None, CM_GROUPS, CHUNK, CHUNK), lambda i: (layer, 0, 0, 0)),
            pl.BlockSpec((None, CM_GROUPS, CHUNK, CHUNK), lambda i: (layer, 0, 0, 0)),
        ],
        out_specs=pl.BlockSpec((tm, CM_WIDTH), lambda i: (i, 0)),
        compiler_params=_cparams(("parallel",), 48),
        name="chunk_mlp",
    )(h, w_cm, cm_nw, wsp, bsp)


def _retproj_kernel(h_ref, w_ref, s_ref, o_ref):
    o_ref[...] = (_dot(h_ref[...], w_ref[...]) * s_ref[...]).astype(BF16)


def _retproj_call(h, w_ret, col_scale, layer):
    tm, tn = 1024, 1024
    n = 2 * RET_QK_WIDTH + 2 * RET_V_WIDTH
    return pl.pallas_call(
        _retproj_kernel,
        out_shape=jax.ShapeDtypeStruct((T, n), BF16),
        grid=(T // tm, n // tn),
        in_specs=[
            pl.BlockSpec((tm, D_MODEL), lambda i, j: (i, 0)),
            pl.BlockSpec((None, D_MODEL, tn), lambda i, j: (layer, 0, W_RET_COL // tn + j)),
            pl.BlockSpec((1, tn), lambda i, j: (0, j)),
        ],
        out_specs=pl.BlockSpec((tm, tn), lambda i, j: (i, j)),
        compiler_params=_cparams(("parallel", "parallel"), 40),
        name="ret_proj",
    )(h, w_ret, col_scale)


def _ret_kernel(*refs, layer, n_chunks, has_s0, emit_state):
    refs = list(refs)
    lg_ref, q_ref, k_ref, v_ref, g_ref, nw_ref = refs[:6]
    pos = 6
    s0_ref = None
    if has_s0:
        s0_ref = refs[pos]
        pos += 1
    pos += 1
    y_ref = refs[pos]
    pos += 1
    so_ref = None
    if emit_state:
        so_ref = refs[pos]
        pos += 1
    yacc_refs = refs[pos:pos + 2]

    head = pl.program_id(1)
    row = lax.broadcasted_iota(jnp.int32, (CHUNK, CHUNK), 0).astype(F32)
    col = lax.broadcasted_iota(jnp.int32, (CHUNK, CHUNK), 1).astype(F32)
    tables = []
    for d in range(2):
        lg = lg_ref[layer * 2 * RET_HEADS + d * RET_HEADS + head]
        if d == 0:
            rel = row - col
            q_decay = jnp.exp(lg * (row + 1.0))
            k_decay = jnp.exp(lg * (CHUNK - 1.0 - row))
        else:
            rel = col - row
            q_decay = jnp.exp(lg * (CHUNK - row))
            k_decay = jnp.exp(lg * row)
        intra = jnp.where(rel >= 0, jnp.exp(lg * jnp.maximum(rel, 0.0)), 0.0)
        chunk_decay = jnp.exp(jnp.full((RET_DK, RET_DV), lg * CHUNK, F32))
        tables.append((intra, q_decay, k_decay, chunk_decay))

    def body(t, states):
        new_states = []
        for d in range(2):
            intra, q_decay, k_decay, chunk_decay = tables[d]
            c = t if d == 0 else n_chunks - 1 - t
            rows = pl.ds(pl.multiple_of(c * CHUNK, CHUNK), CHUNK)
            qc = q_ref[rows, :]
            kc = k_ref[rows, :]
            vc = v_ref[rows, :]
            scores = (_dot_nt(qc, kc) * intra).astype(BF16)
            qd = (qc.astype(F32) * q_decay).astype(BF16)
            kd = (kc.astype(F32) * k_decay).astype(BF16)
            yacc_refs[d][rows, :] = _dot(scores, vc) + _dot(qd, states[d].astype(BF16))
            new_states.append(chunk_decay * states[d] + _dot_tn(kd, vc))
        return tuple(new_states)

    if has_s0:
        s_init = (s0_ref[0], s0_ref[1])
    else:
        s_init = (jnp.zeros((RET_DK, RET_DV), F32), jnp.zeros((RET_DK, RET_DV), F32))
    s_final = lax.fori_loop(0, n_chunks, body, s_init, unroll=min(8, n_chunks))
    if emit_state:
        so_ref[0] = s_final[0]
        so_ref[1] = s_final[1]

    y = yacc_refs[0][...] + yacc_refs[1][...]
    mu = jnp.mean(y, axis=-1, keepdims=True)
    var = jnp.mean(jnp.square(y - mu), axis=-1, keepdims=True)
    yn = (y - mu) * lax.rsqrt(var + EPS) * nw_ref[...]
    y_ref[...] = (jax.nn.silu(g_ref[...].astype(F32)) * yn).astype(BF16)


def _ret_call(lg, rp, ret_nw, s0, y_prev, layer, *, prompt):
    seq = SEQ if prompt else DEC_SEQ
    nb = BATCH if prompt else DEC_BATCH
    rb0 = 0 if prompt else T_P // DEC_SEQ
    kq = RET_QK_WIDTH // RET_DK
    kv = 2 * RET_QK_WIDTH // RET_DV
    kg = kv + RET_HEADS
    in_specs = [
        pl.BlockSpec(memory_space=pltpu.SMEM),
        pl.BlockSpec((seq, RET_DK), lambda b, h: (rb0 + b, h)),
        pl.BlockSpec((seq, RET_DK), lambda b, h: (rb0 + b, kq + h)),
        pl.BlockSpec((seq, RET_DV), lambda b, h: (rb0 + b, kv + h)),
        pl.BlockSpec((seq, RET_DV), lambda b, h: (rb0 + b, kg + h)),
        pl.BlockSpec((None, 1, RET_DV), lambda b, h: (layer, 0, h)),
    ]
    args = [lg, rp, rp, rp, rp, ret_nw]
    if not prompt:
        in_specs.append(pl.BlockSpec((None, None, 2, None, RET_DK, RET_DV), lambda b, h: (b, layer, 0, h, 0, 0)))
        args.append(s0)
    in_specs.append(pl.BlockSpec(memory_space=pl.ANY))
    args.append(y_prev)
    out_shape = [jax.ShapeDtypeStruct((T, RET_V_WIDTH), BF16)]
    out_specs = [pl.BlockSpec((seq, RET_DV), lambda b, h: (rb0 + b, h))]
    if prompt:
        out_shape.append(jax.ShapeDtypeStruct((BATCH, 2, RET_HEADS, RET_DK, RET_DV), F32))
        out_specs.append(pl.BlockSpec((None, 2, None, RET_DK, RET_DV), lambda b, h: (b, 0, h, 0, 0)))
    return pl.pallas_call(
        functools.partial(_ret_kernel, layer=layer, n_chunks=seq // CHUNK, has_s0=not prompt,
                          emit_state=prompt),
        out_shape=tuple(out_shape),
        grid=(nb, RET_HEADS),
        in_specs=in_specs,
        out_specs=tuple(out_specs),
        scratch_shapes=[pltpu.VMEM((seq, RET_DV), F32), pltpu.VMEM((seq, RET_DV), F32)],
        input_output_aliases={len(args) - 1: 0},
        compiler_params=_cparams(("parallel", "parallel"), 32),
        name="retention_prompt" if prompt else "retention_sample",
    )(*args)


def _rope_lanes(x, cos_t, sin_t):
    return x * cos_t + pltpu.roll(x, 64, 1) * sin_t


def _head_keys(kvf, kr2, rotk, wkn, k_ref, v_ref, rows):
    lane = lax.broadcasted_iota(jnp.int32, (1, 2 * MLA_ROPE), 1)
    ssq_kr = jnp.sum(jnp.where(lane < MLA_ROPE, kr2 * kr2, 0.0), axis=-1, keepdims=True)
    nk = MLA_HEADS * MLA_NOPE
    for hh in range(MLA_HEADS):
        kn = kvf[:, hh * MLA_NOPE:(hh + 1) * MLA_NOPE]
        ssq = jnp.sum(kn * kn, axis=-1, keepdims=True) + ssq_kr
        r = lax.rsqrt(ssq / MLA_QK_DIM + EPS)
        k_ref[hh, rows, 0:MLA_NOPE] = (kn * r * wkn).astype(BF16)
        k_ref[hh, rows, MLA_NOPE:MLA_SLAB] = (rotk * r).astype(BF16)
        v_ref[hh, rows, :] = kvf[:, nk + hh * MLA_V: nk + (hh + 1) * MLA_V].astype(BF16)


MLA_ROWS = 256


def _mla_kernel(h_ref, w1_ref, wuq_ref, wukv_ref, qnw_ref, kvnw_ref, wq_ref, wkn_ref, wkr_ref, cos_ref, sin_ref,
                q_ref, k_ref, v_ref, ckv_ref, kr_ref):
    lane = lax.broadcasted_iota(jnp.int32, (1, MLA_SLAB), 1)
    for c in range(h_ref.shape[0] // MLA_ROWS):
        rows = slice(c * MLA_ROWS, (c + 1) * MLA_ROWS)
        p = _dot(h_ref[rows, :], w1_ref[...])
        cq = _rms(p[:, :MLA_Q_RANK]) * qnw_ref[...]
        ckv = _rms(p[:, MLA_Q_RANK:MLA_Q_RANK + MLA_KV_RANK]) * kvnw_ref[...]
        kr2 = p[:, MLA_Q_RANK + MLA_KV_RANK:]
        ckv_ref[rows, :] = ckv
        kr_ref[rows, :] = kr2[:, :MLA_ROPE]
        cos_t = cos_ref[rows, :]
        sin_t = sin_ref[rows, :]
        qf = _dot(cq.astype(BF16), wuq_ref[...])
        kvf = _dot(ckv.astype(BF16), wukv_ref[...])
        for hh in range(MLA_HEADS):
            slab = qf[:, hh * MLA_SLAB:(hh + 1) * MLA_SLAB]
            ssq = jnp.sum(jnp.where(lane < MLA_QK_DIM, slab * slab, 0.0), axis=-1, keepdims=True)
            n = slab * lax.rsqrt(ssq / MLA_QK_DIM + EPS) * wq_ref[...]
            q_ref[hh, rows, 0:MLA_NOPE] = n[:, :MLA_NOPE].astype(BF16)
            q_ref[hh, rows, MLA_NOPE:MLA_SLAB] = _rope_lanes(n[:, MLA_NOPE:], cos_t, sin_t).astype(BF16)
        rotk = _rope_lanes(kr2 * wkr_ref[...], cos_t, sin_t)
        _head_keys(kvf, kr2, rotk, wkn_ref[...], k_ref, v_ref, rows)


ROPE_TILE = 512


def _mla_call(h, w1, wuq, wukv, qnw, kvnw, wq, wkn, wkr, cos_t, sin_t, layer):
    tm = ROPE_TILE
    n1 = MLA_Q_RANK + MLA_KV_RANK + 2 * MLA_ROPE
    lsel = lambda i: (layer, 0, 0)
    n_p = T_P // tm

    def rope_blk(i):
        return (jnp.where(i < n_p, 0, 1 + (i - n_p) % (DEC_SEQ // tm)), 0)

    return pl.pallas_call(
        _mla_kernel,
        out_shape=(
            jax.ShapeDtypeStruct((MLA_HEADS, T, MLA_SLAB), BF16),
            jax.ShapeDtypeStruct((MLA_HEADS, T, MLA_SLAB), BF16),
            jax.ShapeDtypeStruct((MLA_HEADS, T, MLA_V), BF16),
            jax.ShapeDtypeStruct((T, MLA_KV_RANK), F32),
            jax.ShapeDtypeStruct((T, MLA_ROPE), F32),
        ),
        grid=(T // tm,),
        in_specs=[
            pl.BlockSpec((tm, D_MODEL), lambda i: (i, 0)),
            pl.BlockSpec((None, D_MODEL, n1), lsel),
            pl.BlockSpec((None, MLA_Q_RANK, MLA_HEADS * MLA_SLAB), lsel),
            pl.BlockSpec((None, MLA_KV_RANK, MLA_HEADS * (MLA_NOPE + MLA_V)), lsel),
            pl.BlockSpec((None, 1, MLA_Q_RANK), lsel),
            pl.BlockSpec((None, 1, MLA_KV_RANK), lsel),
            pl.BlockSpec((None, 1, MLA_SLAB), lsel),
            pl.BlockSpec((None, 1, MLA_NOPE), lsel),
            pl.BlockSpec((None, 1, 2 * MLA_ROPE), lsel),
            pl.BlockSpec((tm, 2 * MLA_ROPE), rope_blk),
            pl.BlockSpec((tm, 2 * MLA_ROPE), rope_blk),
        ],
        out_specs=(
            pl.BlockSpec((MLA_HEADS, tm, MLA_SLAB), lambda i: (0, i, 0)),
            pl.BlockSpec((MLA_HEADS, tm, MLA_SLAB), lambda i: (0, i, 0)),
            pl.BlockSpec((MLA_HEADS, tm, MLA_V), lambda i: (0, i, 0)),
            pl.BlockSpec((tm, MLA_KV_RANK), lambda i: (i, 0)),
            pl.BlockSpec((tm, MLA_ROPE), lambda i: (i, 0)),
        ),
        compiler_params=_cparams(("parallel",), 48),
        name="mla_proj",
    )(h, w1, wuq, wukv, qnw, kvnw, wq, wkn, wkr, cos_t, sin_t)


def _ctxkv_kernel(c_ref, kr_ref, wukv_ref, wkn_ref, wkr_ref, k_ref, v_ref):
    kvf = _dot(c_ref[...].astype(BF16), wukv_ref[...])
    kr2 = kr_ref[...]
    _head_keys(kvf, kr2, kr2 * wkr_ref[...], wkn_ref[...], k_ref, v_ref, slice(None))


def _ctxkv_call(cache_ckv, cache_kr_pad, wukv, wkn, wkr):
    lsel = lambda l, b: (l, 0, 0)
    return pl.pallas_call(
        _ctxkv_kernel,
        out_shape=(
            jax.ShapeDtypeStruct((DEPTH, MLA_HEADS, DEC_BATCH * PAST_LEN, MLA_SLAB), BF16),
            jax.ShapeDtypeStruct((DEPTH, MLA_HEADS, DEC_BATCH * PAST_LEN, MLA_V), BF16),
        ),
        grid=(DEPTH, DEC_BATCH),
        in_specs=[
            pl.BlockSpec((None, None, PAST_LEN, MLA_KV_RANK), lambda l, b: (b, l, 0, 0)),
            pl.BlockSpec((None, None, PAST_LEN, 2 * MLA_ROPE), lambda l, b: (b, l, 0, 0)),
            pl.BlockSpec((None, MLA_KV_RANK, MLA_HEADS * (MLA_NOPE + MLA_V)), lsel),
            pl.BlockSpec((None, 1, MLA_NOPE), lsel),
            pl.BlockSpec((None, 1, 2 * MLA_ROPE), lsel),
        ],
        out_specs=(
            pl.BlockSpec((None, MLA_HEADS, PAST_LEN, MLA_SLAB), lambda l, b: (l, 0, b, 0)),
            pl.BlockSpec((None, MLA_HEADS, PAST_LEN, MLA_V), lambda l, b: (l, 0, b, 0)),
        ),
        compiler_params=_cparams(("parallel", "parallel"), 32),
        name="ctx_kv",
    )(cache_ckv, cache_kr_pad, wukv, wkn, wkr)


def _attend(q, keys, values):
    scores = [_dot_nt(q, k) for k in keys]
    m = functools.reduce(jnp.maximum, [jnp.max(s, axis=-1, keepdims=True) for s in scores])
    probs = [jnp.exp2(s - m).astype(BF16) for s in scores]
    o = sum(_dot(p, v) for p, v in zip(probs, values))
    return (o[:, :MLA_V] / o[:, MLA_V:]).astype(BF16)


def _with_ones(v):
    return jnp.concatenate([v, jnp.ones_like(v)], axis=-1)


ATTN_ROWS = 256


def _attn_prompt_kernel(q_ref, k_ref, v_ref, _, o_ref):
    for hh in range(MLA_HEADS):
        o_ref[:, hh * MLA_V:(hh + 1) * MLA_V] = _attend(q_ref[hh], [k_ref[hh]], [_with_ones(v_ref[hh])])


def _attn_sample_kernel(q_ref, ks_ref, vs_ref, kc_ref, vc_ref, _, o_ref):
    keys = [ks_ref[...], kc_ref[...]]
    values = [_with_ones(vs_ref[...]), _with_ones(vc_ref[...])]
    for r in range(q_ref.shape[0] // ATTN_ROWS):
        rows = slice(r * ATTN_ROWS, (r + 1) * ATTN_ROWS)
        o_ref[rows, :] = _attend(q_ref[rows, :], keys, values)


def _attn_prompt_call(q, k, v, y_prev):
    return pl.pallas_call(
        _attn_prompt_kernel,
        out_shape=jax.ShapeDtypeStruct((T, MLA_HEADS * MLA_V), BF16),
        grid=(BATCH,),
        in_specs=[
            pl.BlockSpec((MLA_HEADS, SEQ, MLA_SLAB), lambda b: (0, b, 0)),
            pl.BlockSpec((MLA_HEADS, SEQ, MLA_SLAB), lambda b: (0, b, 0)),
            pl.BlockSpec((MLA_HEADS, SEQ, MLA_V), lambda b: (0, b, 0)),
            pl.BlockSpec(memory_space=pl.ANY),
        ],
        out_specs=pl.BlockSpec((SEQ, MLA_HEADS * MLA_V), lambda b: (b, 0)),
        input_output_aliases={3: 0},
        compiler_params=_cparams(("parallel",), 32),
        name="attn_prompt",
    )(q, k, v, y_prev)


def _attn_sample_call(q, k, v, kc, vc, y_prev, layer):
    tq = 2048
    nq = DEC_SEQ // tq
    q0 = T_P // tq
    s0 = T_P // DEC_SEQ
    return pl.pallas_call(
        _attn_sample_kernel,
        out_shape=jax.ShapeDtypeStruct((T, MLA_HEADS * MLA_V), BF16),
        grid=(DEC_BATCH, MLA_HEADS, nq),
        in_specs=[
            pl.BlockSpec((None, tq, MLA_SLAB), lambda b, h, i: (h, q0 + b * nq + i, 0)),
            pl.BlockSpec((None, DEC_SEQ, MLA_SLAB), lambda b, h, i: (h, s0 + b, 0)),
            pl.BlockSpec((None, DEC_SEQ, MLA_V), lambda b, h, i: (h, s0 + b, 0)),
            pl.BlockSpec((None, None, PAST_LEN, MLA_SLAB), lambda b, h, i: (layer, h, b, 0)),
            pl.BlockSpec((None, None, PAST_LEN, MLA_V), lambda b, h, i: (layer, h, b, 0)),
            pl.BlockSpec(memory_space=pl.ANY),
        ],
        out_specs=pl.BlockSpec((tq, MLA_V), lambda b, h, i: (q0 + b * nq + i, h)),
        input_output_aliases={5: 0},
        compiler_params=_cparams(("parallel", "parallel", "parallel"), 48),
        name="attn_sample",
    )(q, k, v, kc, vc, y_prev)


def _merge_kernel(h_ref, ya_ref, yb_ref, yc_ref, wga_ref, wgb_ref, wgc_ref, wb_ref, o_ref):
    sub = h_ref.shape[0] // 2
    for s in range(2):
        rows = slice(s * sub, (s + 1) * sub)
        h = h_ref[rows, :]
        m = jax.nn.sigmoid(_dot(h, wga_ref[...])) * _dot(ya_ref[rows, :], wb_ref[0])
        m = m + jax.nn.sigmoid(_dot(h, wgb_ref[...])) * _dot(yb_ref[rows, :], wb_ref[1])
        m = m + jax.nn.sigmoid(_dot(h, wgc_ref[...])) * _dot(yc_ref[rows, :], wb_ref[2])
        o_ref[rows, :] = m.astype(BF16)


def _merge_call(h, ya, yb, yc, w_g, w_b, layer):
    tm, tn = 1024, 512
    nj = D_MODEL // tn
    yspec = pl.BlockSpec((tm, BRANCH_WIDTH), lambda i, j: (i, 0))
    return pl.pallas_call(
        _merge_kernel,
        out_shape=jax.ShapeDtypeStruct((T, D_MODEL), BF16),
        grid=(T // tm, nj),
        in_specs=[
            pl.BlockSpec((tm, D_MODEL), lambda i, j: (i, 0)),
            yspec, yspec, yspec,
            pl.BlockSpec((None, D_MODEL, tn), lambda i, j: (layer, 0, j)),
            pl.BlockSpec((None, D_MODEL, tn), lambda i, j: (layer, 0, j + nj)),
            pl.BlockSpec((None, D_MODEL, tn), lambda i, j: (layer, 0, j + 2 * nj)),
            pl.BlockSpec((None, 3, BRANCH_WIDTH, tn), lambda i, j: (layer, 0, 0, j)),
        ],
        out_specs=pl.BlockSpec((tm, tn), lambda i, j: (i, j)),
        compiler_params=_cparams(("parallel", "parallel"), 56),
        name="gated_merge",
    )(h, ya, yb, yc, w_g, w_g, w_g, w_b)


def _outproj_kernel(m_ref, w_ref, x_ref, g_ref, sh_ref, sc_ref, xo_ref, ho_ref):
    xn = x_ref[...] + g_ref[...] * _dot(m_ref[...], w_ref[...])
    xo_ref[...] = xn
    ho_ref[...] = _norm_mod(xn, sh_ref[...], sc_ref[...]).astype(BF16)


def _outproj_call(m, w_out, x, mod5, layer):
    tm = 512
    return pl.pallas_call(
        _outproj_kernel,
        out_shape=(jax.ShapeDtypeStruct((T, D_MODEL), F32), jax.ShapeDtypeStruct((T, D_MODEL), BF16)),
        grid=(T // tm,),
        in_specs=[
            pl.BlockSpec((tm, D_MODEL), lambda i: (i, 0)),
            pl.BlockSpec((None, D_MODEL, D_MODEL), lambda i: (layer, 0, 0)),
            pl.BlockSpec((tm, D_MODEL), lambda i: (i, 0)),
            _mod_spec(layer, 5, tm),
            _mod_spec(layer, 6, tm),
            _mod_spec(layer, 7, tm),
        ],
        out_specs=(pl.BlockSpec((tm, D_MODEL), lambda i: (i, 0)),
                   pl.BlockSpec((tm, D_MODEL), lambda i: (i, 0))),
        compiler_params=_cparams(("parallel",), 52),
        name="out_proj",
    )(m, w_out, x, mod5, mod5, mod5)


def _swap_halves(x):
    half = x.shape[-1] // 2
    return jnp.concatenate([x[..., half:], x[..., :half]], axis=-1)


def _rope_tables():
    t = jnp.arange(DEC_SEQ)
    row = (t // GRID_W).astype(F32)
    col = (t % GRID_W).astype(F32)
    n_freq = MLA_ROPE // 4
    freqs = jnp.power(ROPE_THETA, -jnp.arange(n_freq, dtype=F32) / n_freq)
    ang = jnp.concatenate([row[:, None] * freqs, col[:, None] * freqs], axis=-1)
    cos, sin = jnp.cos(ang), jnp.sin(ang)
    zeros = jnp.zeros((DEC_SEQ, MLA_ROPE), F32)
    cos_s = jnp.concatenate([cos, cos, zeros], axis=-1)
    sin_s = jnp.concatenate([-sin, sin, zeros], axis=-1)
    cos_p = jnp.concatenate([jnp.ones((ROPE_TILE, MLA_ROPE), F32), jnp.zeros((ROPE_TILE, MLA_ROPE), F32)], axis=-1)
    sin_p = jnp.zeros((ROPE_TILE, 2 * MLA_ROPE), F32)
    return jnp.concatenate([cos_p, cos_s], axis=0), jnp.concatenate([sin_p, sin_s], axis=0)


def kernel(x_prompt, x_sample, cache_mla_ckv, cache_mla_krope, state_retention, c, c_ctx, w_mod, b_mod,
           ffn1_w_gate_up, ffn1_w_down, ffn2_w_gate_up, ffn2_w_down, w_in, cm_norm_w, cm_w_spatial,
           cm_b_spatial, ret_decay_logit, ret_norm_w, mla_q_norm_w, mla_kv_norm_w, mla_w_uq, mla_w_ukv,
           qk_norm_q_w, qk_norm_k_w, w_branch, w_out):
    w_main = w_in.astype(BF16)
    w_gate = w_main[:, :, W_GATE_COL:]
    w_kr = w_main[:, :, W_KR_COL:W_GATE_COL]
    w_mla1 = jnp.concatenate([w_main[:, :, W_MLA_COL:W_KR_COL], w_kr, _swap_halves(w_kr)], axis=-1)
    w_b = w_branch.astype(BF16)
    w_o = w_out.astype(BF16)

    uq = mla_w_uq.reshape(DEPTH, MLA_Q_RANK, MLA_HEADS, MLA_QK_DIM)
    uq_rope = uq[..., MLA_NOPE:]
    wuq = jnp.concatenate([uq[..., :MLA_NOPE], uq_rope, _swap_halves(uq_rope)], axis=-1)
    wuq = wuq.reshape(DEPTH, MLA_Q_RANK, MLA_HEADS * MLA_SLAB).astype(BF16)
    ukv = mla_w_ukv.reshape(DEPTH, MLA_KV_RANK, MLA_HEADS, MLA_NOPE + MLA_V)
    wukv = jnp.concatenate([ukv[..., :MLA_NOPE].reshape(DEPTH, MLA_KV_RANK, MLA_HEADS * MLA_NOPE),
                            ukv[..., MLA_NOPE:].reshape(DEPTH, MLA_KV_RANK, MLA_HEADS * MLA_V)],
                           axis=-1).astype(BF16)
    q_rope_w = qk_norm_q_w[:, MLA_NOPE:]
    wq = (jnp.concatenate([qk_norm_q_w[:, :MLA_NOPE], q_rope_w, _swap_halves(q_rope_w)], axis=-1)
          * (MLA_QK_DIM ** -0.5 * LOG2_E)).reshape(DEPTH, 1, MLA_SLAB)
    k_rope_w = qk_norm_k_w[:, MLA_NOPE:]
    wkn = qk_norm_k_w[:, :MLA_NOPE].reshape(DEPTH, 1, MLA_NOPE)
    wkr = jnp.concatenate([k_rope_w, _swap_halves(k_rope_w)], axis=-1).reshape(DEPTH, 1, 2 * MLA_ROPE)
    qnw = mla_q_norm_w.reshape(DEPTH, 1, MLA_Q_RANK)
    kvnw = mla_kv_norm_w.reshape(DEPTH, 1, MLA_KV_RANK)
    cm_nw = cm_norm_w.reshape(DEPTH, 1, CM_WIDTH)
    wsp = cm_w_spatial.astype(BF16)
    bsp = jnp.broadcast_to(cm_b_spatial[..., None], (DEPTH, CM_GROUPS, CHUNK, CHUNK))
    ret_nw = ret_norm_w.reshape(DEPTH, 1, RET_V_WIDTH)
    lg = jax.nn.log_sigmoid(ret_decay_logit.astype(F32)).reshape(-1)
    ret_scale = jnp.concatenate([jnp.ones((RET_QK_WIDTH,), F32), jnp.full((RET_QK_WIDTH,), RET_DK ** -0.5, F32),
                                 jnp.ones((2 * RET_V_WIDTH,), F32)]).reshape(1, -1)
    cos_t, sin_t = _rope_tables()
    cache_kr_pad = jnp.pad(cache_mla_krope, ((0, 0), (0, 0), (0, 0), (0, MLA_ROPE)))

    cond = jnp.concatenate([c_ctx[None, :], c, jnp.zeros((MOD_ROWS - 1 - DEC_BATCH, D_MODEL), F32)], axis=0)

    mod5 = _mod_call(cond, w_mod, b_mod).reshape(DEPTH, MOD_ROWS, N_MOD, 1, D_MODEL)
    kc, vc = _ctxkv_call(cache_mla_ckv, cache_kr_pad, wukv, wkn, wkr)
    x, h = _prenorm_call(x_prompt, x_sample, mod5)

    yb = jnp.zeros((T, RET_V_WIDTH), BF16)
    yc = jnp.zeros((T, MLA_HEADS * MLA_V), BF16)
    ckv_layers, krope_layers, ret_layers = [], [], []
    for l in range(DEPTH):
        a, w_d = _ffn_up_call(h, ffn1_w_gate_up, ffn1_w_down, l)
        x, h = _ffn_down_call(a, w_d, x, mod5, l, 2, next_mod=(l, 3))

        ya = _cm_call(h, w_main, cm_nw, wsp, bsp, l)

        rp = _retproj_call(h, w_main, ret_scale, l)
        yb, ret_state = _ret_call(lg, rp, ret_nw, None, yb, l, prompt=True)
        (yb,) = _ret_call(lg, rp, ret_nw, state_retention, yb, l, prompt=False)

        q, k, v, ckv, kr = _mla_call(h, w_mla1, wuq, wukv, qnw, kvnw, wq, wkn, wkr, cos_t, sin_t, l)
        yc = _attn_prompt_call(q, k, v, yc)
        yc = _attn_sample_call(q, k, v, kc, vc, yc, l)

        merged = _merge_call(h, ya, yb, yc, w_gate, w_b, l)
        x, h = _outproj_call(merged, w_o, x, mod5, l)

        a, w_d = _ffn_up_call(h, ffn2_w_gate_up, ffn2_w_down, l)
        if l + 1 < DEPTH:
            x, h = _ffn_down_call(a, w_d, x, mod5, l, 8, next_mod=(l + 1, 0))
        else:
            (y_prompt,) = _ffn_down_call(a, w_d, x, mod5, l, 8, rows=(0, T_P))
            (y_sample,) = _ffn_down_call(a, w_d, x, mod5, l, 8, rows=(T_P, T))

        ckv_layers.append(ckv[:T_P].reshape(BATCH, SEQ, MLA_KV_RANK))
        krope_layers.append(kr[:T_P].reshape(BATCH, SEQ, MLA_ROPE))
        ret_layers.append(ret_state)

    return (y_prompt.reshape(BATCH, SEQ, D_MODEL), y_sample.reshape(DEC_BATCH, DEC_SEQ, D_MODEL),
            jnp.stack(ckv_layers, axis=1), jnp.stack(krope_layers, axis=1), jnp.stack(ret_layers, axis=1))
```
